```python
import math
import jax, jax.numpy as jnp
from jax import lax
import numpy as np

D_MODEL = 2048
BATCH = 4
SEQ = 4096
DEPTH = 2

D_SSM = 1024
SSM_GROUP = 16
N_GROUPS = D_SSM // SSM_GROUP
STATE = 64
DT_MIN = 1e-3
DT_MAX = 1e-1
D_ATTN = 1024
HEAD_DIM = 128
N_HEADS = D_ATTN // HEAD_DIM
MOBA_BLOCK = 256
MOBA_TOPK = 3
Q_CHUNK = 16
D_FF = 5632
CONV_W = 3
EPS = 1e-6
NEG = -1e30
D_IN = D_SSM + 3 * D_ATTN + 2 * D_MODEL

kernel_name = "hybrid_s5_moba_convffn_block"


def rmsnorm(x, g):
    xf = x.astype(jnp.float32)
    y = xf * lax.rsqrt(jnp.mean(xf * xf, axis=-1, keepdims=True) + EPS)
    return (y * g.astype(jnp.float32)).astype(x.dtype)


def s5_mixer(u, lam_re, lam_im, log_dt, b_re, b_im, c_re, c_im, d_skip, w_glu):
    bsz, s, _ = u.shape
    f32 = jnp.float32
    lam = lax.complex(lam_re.astype(f32), lam_im.astype(f32))
    dt = jnp.exp(log_dt.astype(f32))[:, None]
    lam_bar = jnp.exp(lam * dt)
    b = lax.complex(b_re.astype(f32), b_im.astype(f32))
    b_bar = ((lam_bar - 1.0) / lam)[..., None] * b
    c = lax.complex(c_re.astype(f32), c_im.astype(f32))
    ug = u.astype(f32).reshape(bsz, s, N_GROUPS, SSM_GROUP)

    def combine(left, right):
        a_l, h_l = left
        a_r, h_r = right
        return a_r * a_l, a_r * h_l + h_r

    def scan_one(u_seq):
        bu = jnp.einsum('gpm,sgm->sgp', b_bar, u_seq.astype(jnp.complex64))
        a = jnp.broadcast_to(lam_bar, bu.shape)
        _, h = lax.associative_scan(combine, (a, bu), axis=0)
        return jnp.einsum('gmp,sgp->sgm', c, h).real

    y = lax.map(scan_one, ug)
    y = y + d_skip.astype(f32).reshape(N_GROUPS, SSM_GROUP) * ug
    y = jax.nn.gelu(y.reshape(bsz, s, D_SSM))
    y = y * jax.nn.sigmoid(y @ w_glu.astype(f32))
    return y.astype(u.dtype)


def moba_attention(q, k, v):
    f32 = jnp.float32
    bsz, s, nh, dh = q.shape
    s_pad = -(-s // MOBA_BLOCK) * MOBA_BLOCK
    pad = s_pad - s
    q, k, v = [jnp.pad(t, ((0, 0), (0, pad), (0, 0), (0, 0))).transpose(0, 2, 1, 3) for t in (q, k, v)]
    nb = s_pad // MOBA_BLOCK
    topk = min(MOBA_TOPK, nb)
    L = MOBA_BLOCK
    kb = k.reshape(bsz, nh, nb, L, dh)
    vb = v.reshape(bsz, nh, nb, L, dh)

    kmean = jnp.mean(kb.astype(f32), axis=3)
    gate = jnp.einsum('bhsd,bhnd->bhsn', q.astype(f32), kmean)
    qblk = jnp.arange(s_pad) // L
    past = jnp.arange(nb)[None, :] < qblk[:, None]
    gate = jnp.where(past, gate, NEG)
    _, sel = lax.top_k(gate, topk)
    valid = jnp.arange(topk)[None, :] < qblk[:, None]

    nc = s_pad // Q_CHUNK
    scale = 1.0 / math.sqrt(dh)

    def to_chunks(t):
        t = t.reshape(bsz, nh, nc, Q_CHUNK, *t.shape[3:])
        return jnp.moveaxis(t, 2, 0)

    q_c = to_chunks(q)
    sel_c = to_chunks(sel)
    valid_c = valid.reshape(nc, Q_CHUNK, topk)
    b_ix = jnp.arange(bsz)[:, None, None, None]
    h_ix = jnp.arange(nh)[None, :, None, None]

    def attend(args):
        ci, qc, selc, validc = args
        q0 = ci * Q_CHUNK
        qpos = q0 + jnp.arange(Q_CHUNK)
        own = q0 // L
        kg = kb[b_ix, h_ix, selc].astype(f32)
        vg = vb[b_ix, h_ix, selc].astype(f32)
        k_own = lax.dynamic_index_in_dim(kb, own, axis=2, keepdims=False).astype(f32)
        v_own = lax.dynamic_index_in_dim(vb, own, axis=2, keepdims=False).astype(f32)
        qf = qc.astype(f32) * scale
        s_sel = jnp.einsum('bhqd,bhqkld->bhqkl', qf, kg)
        s_sel = jnp.where(validc[None, None, :, :, None], s_sel, NEG)
        s_own = jnp.einsum('bhqd,bhld->bhql', qf, k_own)
        kpos = own * L + jnp.arange(L)
        s_own = jnp.where(kpos[None, :] <= qpos[:, None], s_own, NEG)
        scores = jnp.concatenate([s_sel.reshape(bsz, nh, Q_CHUNK, topk * L), s_own], axis=-1)
        p = jax.nn.softmax(scores, axis=-1)
        p_sel = p[..., :topk * L].reshape(bsz, nh, Q_CHUNK, topk, L)
        p_own = p[..., topk * L:]
        o = jnp.einsum('bhqkl,bhqkld->bhqd', p_sel, vg) + jnp.einsum('bhql,bhld->bhqd', p_own, v_own)
        return o.astype(qc.dtype)

    out = lax.map(attend, (jnp.arange(nc), q_c, sel_c, valid_c))
    out = jnp.moveaxis(out, 0, 2).reshape(bsz, nh, s_pad, dh)[:, :, :s]
    return out.transpose(0, 2, 1, 3).reshape(bsz, s, nh * dh)


def conv_ffn(h, w_up, conv_w, conv_b, w_down):
    z = h @ w_up
    z = lax.conv_general_dilated(
        z, conv_w[:, None, :], window_strides=(1,), padding=[(CONV_W - 1, 0)],
        dimension_numbers=('NWC', 'WIO', 'NWC'), feature_group_count=z.shape[-1]) + conv_b
    a, val = jnp.split(z, 2, axis=-1)
    return (jax.nn.silu(a) * val) @ w_down


def hybrid_layer(x, g_pre_mix, w_in, lam_re, lam_im, log_dt, b_re, b_im, c_re, c_im, d_skip, w_glu,
                 w_up_ssm, w_up_attn, w_out, g_post_mix, g_pre_ffn, w_ffn_up, conv_w, conv_b,
                 w_ffn_down, g_post_ffn):
    bsz, s, _ = x.shape
    h = rmsnorm(x, g_pre_mix)
    proj = h @ w_in
    cuts = [D_SSM, D_SSM + D_ATTN, D_SSM + 2 * D_ATTN, D_SSM + 3 * D_ATTN, D_SSM + 3 * D_ATTN + D_MODEL]
    u, q, k, v, ga, gb = jnp.split(proj, cuts, axis=-1)
    y_ssm = s5_mixer(u, lam_re, lam_im, log_dt, b_re, b_im, c_re, c_im, d_skip, w_glu) @ w_up_ssm
    hd = (bsz, s, N_HEADS, HEAD_DIM)
    y_att = moba_attention(q.reshape(hd), k.reshape(hd), v.reshape(hd)) @ w_up_attn
    m = jax.nn.sigmoid(ga) * y_ssm + jax.nn.sigmoid(gb) * y_att
    x = x + rmsnorm(m @ w_out, g_post_mix)
    f = conv_ffn(rmsnorm(x, g_pre_ffn), w_ffn_up, conv_w, conv_b, w_ffn_down)
    return x + rmsnorm(f, g_post_ffn)


def setup_inputs(seed: int = 0) -> dict:
    key = jax.random.key(seed)
    ks = jax.random.split(key, 24)
    f32 = jnp.float32
    nrm = lambda k, shp, sc: jax.random.normal(k, shp, f32) * sc
    gain = lambda k: 1.0 + 0.05 * jax.random.normal(k, (DEPTH, D_MODEL), f32)
    lam_im = jnp.broadcast_to(jnp.pi * jnp.arange(STATE, dtype=f32), (DEPTH, N_GROUPS, STATE))
    return {
        'x': jax.random.normal(ks[0], (BATCH, SEQ, D_MODEL), f32),
        'g_pre_mix': gain(ks[1]),
        'w_in': nrm(ks[2], (DEPTH, D_MODEL, D_IN), D_MODEL ** -0.5),
        'lam_re': -0.5 + nrm(ks[3], (DEPTH, N_GROUPS, STATE), 0.01),
        'lam_im': lam_im + nrm(ks[4], (DEPTH, N_GROUPS, STATE), 0.01),
        'log_dt': jax.random.uniform(ks[5], (DEPTH, N_GROUPS), f32, math.log(DT_MIN), math.log(DT_MAX)),
        'b_re': nrm(ks[6], (DEPTH, N_GROUPS, STATE, SSM_GROUP), (2 * SSM_GROUP) ** -0.5),
        'b_im': nrm(ks[7], (DEPTH, N_GROUPS, STATE, SSM_GROUP), (2 * SSM_GROUP) ** -0.5),
        'c_re': nrm(ks[8], (DEPTH, N_GROUPS, SSM_GROUP, STATE), (2 * STATE) ** -0.5),
        'c_im': nrm(ks[9], (DEPTH, N_GROUPS, SSM_GROUP, STATE), (2 * STATE) ** -0.5),
        'd_skip': nrm(ks[10], (DEPTH, D_SSM), 1.0),
        'w_glu': nrm(ks[11], (DEPTH, D_SSM, D_SSM), D_SSM ** -0.5),
        'w_up_ssm': nrm(ks[12], (DEPTH, D_SSM, D_MODEL), D_SSM ** -0.5),
        'w_up_attn': nrm(ks[13], (DEPTH, D_ATTN, D_MODEL), D_ATTN ** -0.5),
        'w_out': nrm(ks[14], (DEPTH, D_MODEL, D_MODEL), D_MODEL ** -0.5),
        'g_post_mix': gain(ks[15]),
        'g_pre_ffn': gain(ks[16]),
        'w_ffn_up': nrm(ks[17], (DEPTH, D_MODEL, 2 * D_FF), D_MODEL ** -0.5),
        'conv_w': nrm(ks[18], (DEPTH, CONV_W, 2 * D_FF), CONV_W ** -0.5),
        'conv_b': nrm(ks[19], (DEPTH, 2 * D_FF), 0.01),
        'w_ffn_down': nrm(ks[20], (DEPTH, D_FF, D_MODEL), D_FF ** -0.5),
        'g_post_ffn': gain(ks[21]),
    }


def reference(x, g_pre_mix, w_in, lam_re, lam_im, log_dt, b_re, b_im, c_re, c_im, d_skip, w_glu,
              w_up_ssm, w_up_attn, w_out, g_post_mix, g_pre_ffn, w_ffn_up, conv_w, conv_b,
              w_ffn_down, g_post_ffn):
    for l in range(DEPTH):
        x = hybrid_layer(x, g_pre_mix[l], w_in[l], lam_re[l], lam_im[l], log_dt[l], b_re[l], b_im[l],
                         c_re[l], c_im[l], d_skip[l], w_glu[l], w_up_ssm[l], w_up_attn[l], w_out[l],
                         g_post_mix[l], g_pre_ffn[l], w_ffn_up[l], conv_w[l], conv_b[l],
                         w_ffn_down[l], g_post_ffn[l])
    return x
```

```python
import functools
import math

import jax
import jax.numpy as jnp
from jax import lax
from jax.experimental import pallas as pl
from jax.experimental.pallas import tpu as pltpu

F32 = jnp.float32
BF16 = jnp.bfloat16

EPS = 1e-6
NEG = -1e30
SSM_GROUP = 16
SSM_STATE = 64
HEAD_DIM = 128
MOBA_BLOCK = 256
MOBA_TOPK = 3
CONV_W = 3

LANES = 128
SUBLANES = 8
VMEM_LIMIT_BYTES = 56 * 1024 * 1024

SSM_CHUNK = 32
ROW_TILE_PROJ = 1024
COL_TILE_PROJ = 512
ROW_TILE_GATE = 512
ROW_TILE_MERGE = 512
ROW_TILE_FFN = 512
COL_TILE_FFN = 512
K_TILE_GATE = 512

_NT = (((1,), (1,)), ((), ()))


def _params(n_axes):
    return pltpu.CompilerParams(
        dimension_semantics=("arbitrary",) * n_axes,
        vmem_limit_bytes=VMEM_LIMIT_BYTES,
    )


def _rmsnorm_rows(x, g):
    ms = jnp.mean(x * x, axis=-1, keepdims=True)
    return x * lax.rsqrt(ms + EPS) * g


NORM_ROWS = 128


def _rmsnorm_into(h_ref, x_ref, g_ref):
    step_rows = min(NORM_ROWS, x_ref.shape[0])

    def step(r, carry):
        r0 = pl.multiple_of(r * step_rows, step_rows)
        h_ref[pl.ds(r0, step_rows), :] = _rmsnorm_rows(
            x_ref[pl.ds(r0, step_rows), :], g_ref[...]).astype(h_ref.dtype)
        return carry

    lax.fori_loop(0, x_ref.shape[0] // step_rows, step, 0)


def _inproj_kernel(x_ref, g_ref, w_ref, cs_ref, o_ref, h_ref):
    @pl.when(pl.program_id(1) == 0)
    def _():
        _rmsnorm_into(h_ref, x_ref, g_ref)

    acc = jnp.dot(h_ref[...], w_ref[...], preferred_element_type=F32)
    o_ref[...] = (acc * cs_ref[...]).astype(o_ref.dtype)


def _inproj(x2, g, w_bf, col_scale):
    n_rows, d = x2.shape
    d_in = w_bf.shape[1]
    tm = min(ROW_TILE_PROJ, n_rows)
    tn = COL_TILE_PROJ
    return pl.pallas_call(
        _inproj_kernel,
        grid=(n_rows // tm, d_in // tn),
        in_specs=[
            pl.BlockSpec((tm, d), lambda i, j: (i, 0)),
            pl.BlockSpec((1, d), lambda i, j: (0, 0)),
            pl.BlockSpec((d, tn), lambda i, j: (0, j)),
            pl.BlockSpec((1, tn), lambda i, j: (0, j)),
        ],
        out_specs=pl.BlockSpec((tm, tn), lambda i, j: (i, j)),
        out_shape=jax.ShapeDtypeStruct((n_rows, d_in), BF16),
        scratch_shapes=[pltpu.VMEM((tm, d), BF16)],
        compiler_params=_params(2),
        name="inproj",
    )(x2, g, w_bf, col_scale)


def _blockmean_kernel(x_ref, g_ref, o_ref):
    h = _rmsnorm_rows(x_ref[...], g_ref[...])
    o_ref[0] = jnp.mean(h, axis=0, keepdims=True)


def _blockmean(x2, g):
    n_rows, d = x2.shape
    n_blocks = n_rows // MOBA_BLOCK
    return pl.pallas_call(
        _blockmean_kernel,
        grid=(n_blocks,),
        in_specs=[
            pl.BlockSpec((MOBA_BLOCK, d), lambda i: (i, 0)),
            pl.BlockSpec((1, d), lambda i: (0, 0)),
        ],
        out_specs=pl.BlockSpec((1, 1, d), lambda i: (i, 0, 0)),
        out_shape=jax.ShapeDtypeStruct((n_blocks, 1, d), F32),
        compiler_params=_params(1),
        name="blockmean",
    )(x2, g)


def _kmean_kernel(hb_ref, wk_ref, o_ref, *, nb):
    @pl.when(pl.program_id(0) == 0)
    def _():
        o_ref[...] = jnp.zeros_like(o_ref)

    o_ref[...] += jnp.dot(hb_ref[...], wk_ref[...], preferred_element_type=F32,
                          precision=lax.Precision.HIGHEST)

    @pl.when(pl.program_id(0) == pl.num_programs(0) - 1)
    def _():
        r = lax.broadcasted_iota(jnp.int32, o_ref.shape, 0)
        c = lax.broadcasted_iota(jnp.int32, o_ref.shape, 1)
        n_gate = (o_ref.shape[1] // HEAD_DIM) * nb
        keep = (c // HEAD_DIM) == ((r % n_gate) // nb)
        o_ref[...] = jnp.where(keep, o_ref[...], 0.0)


def _kmean(hbar_t, w_in, nb, d_ssm, d_attn):
    rows, d = hbar_t.shape
    k_col_block = (d_ssm + d_attn) // d_attn
    tk = K_TILE_GATE
    return pl.pallas_call(
        functools.partial(_kmean_kernel, nb=nb),
        grid=(d // tk,),
        in_specs=[
            pl.BlockSpec((rows, tk), lambda kk: (0, kk)),
            pl.BlockSpec((tk, d_attn), lambda kk: (kk, k_col_block)),
        ],
        out_specs=pl.BlockSpec((rows, d_attn), lambda kk: (0, 0)),
        out_shape=jax.ShapeDtypeStruct((rows, d_attn), F32),
        compiler_params=_params(1),
        name="kmean",
    )(hbar_t, w_in)


def _gatevec_kernel(wq_ref, kbd_ref, o_ref):
    o_ref[0] = lax.dot_general(wq_ref[...], kbd_ref[0], _NT, preferred_element_type=F32,
                               precision=lax.Precision.HIGHEST)


def _gatevec(w_in, kbd, d_ssm):
    bsz, n_gate, d_attn = kbd.shape
    d = w_in.shape[0]
    q_col_block = d_ssm // d_attn
    tk = K_TILE_GATE
    return pl.pallas_call(
        _gatevec_kernel,
        grid=(bsz, d // tk),
        in_specs=[
            pl.BlockSpec((tk, d_attn), lambda b, kk: (kk, q_col_block)),
            pl.BlockSpec((1, n_gate, d_attn), lambda b, kk: (b, 0, 0)),
        ],
        out_specs=pl.BlockSpec((1, tk, n_gate), lambda b, kk: (b, kk, 0)),
        out_shape=jax.ShapeDtypeStruct((bsz, d, n_gate), F32),
        compiler_params=_params(2),
        name="gatevec",
    )(w_in, kbd)


def _select_kernel(x_ref, g_ref, z_ref, o_ref, *, nb, tq):
    h = _rmsnorm_rows(x_ref[0], g_ref[...])
    gate = jnp.dot(h, z_ref[0], preferred_element_type=F32, precision=lax.Precision.HIGHEST)
    n_gate = gate.shape[1]
    row = lax.broadcasted_iota(jnp.int32, gate.shape, 0) + pl.program_id(1) * tq
    qblk = row // MOBA_BLOCK
    j = lax.broadcasted_iota(jnp.int32, gate.shape, 1) % nb
    past = j < qblk
    gate = jnp.where(past, gate, NEG)
    cnt = jnp.zeros(gate.shape, F32)
    for r in range(1, nb):
        lower = pltpu.roll(gate, r, 1)
        upper = pltpu.roll(gate, (r - nb) % n_gate, 1)
        has_lower = j >= r
        other = jnp.where(has_lower, lower, upper)
        wins_tie = jnp.where(other >= gate, 1.0, 0.0)
        wins_strict = jnp.where(other > gate, 1.0, 0.0)
        cnt = cnt + jnp.where(has_lower, wins_tie, wins_strict)
    keep = jnp.logical_and(past, cnt < float(MOBA_TOPK))
    o_ref[0] = jnp.where(keep, 0.0, NEG)


def _select(x3, g, z, nb):
    bsz, s, d = x3.shape
    n_gate = z.shape[2]
    tq = min(ROW_TILE_GATE, s)
    return pl.pallas_call(
        functools.partial(_select_kernel, nb=nb, tq=tq),
        grid=(bsz, s // tq),
        in_specs=[
            pl.BlockSpec((1, tq, d), lambda b, i: (b, i, 0)),
            pl.BlockSpec((1, d), lambda b, i: (0, 0)),
            pl.BlockSpec((1, d, n_gate), lambda b, i: (b, 0, 0)),
        ],
        out_specs=pl.BlockSpec((1, tq, n_gate), lambda b, i: (b, i, 0)),
        out_shape=jax.ShapeDtypeStruct((bsz, s, n_gate), F32),
        compiler_params=_params(2),
        name="select",
    )(x3, g, z)


def _attn_kernel(q_ref, k_ref, v_ref, bias_ref, o_ref, qa_ref, *, nb):
    hd = pl.program_id(1)
    i = pl.program_id(2)
    blk = MOBA_BLOCK
    q = q_ref[0]
    n_gate = bias_ref.shape[2]
    qa_ref[:, :HEAD_DIM] = q
    qa_ref[:, HEAD_DIM:] = bias_ref[0].astype(BF16)

    start = pl.multiple_of(i * blk, blk)
    kd = k_ref[0, pl.ds(start, blk), :]
    vd = v_ref[0, pl.ds(start, blk), :]
    s = lax.dot_general(q, kd, _NT, preferred_element_type=F32)
    row = lax.broadcasted_iota(jnp.int32, s.shape, 0)
    col = lax.broadcasted_iota(jnp.int32, s.shape, 1)
    s = jnp.where(col <= row, s, NEG)
    m0 = jnp.max(s, axis=-1, keepdims=True)
    p = jnp.exp(s - m0)
    l0 = jnp.sum(p, axis=-1, keepdims=True)
    acc0 = jnp.dot(p.astype(BF16), vd, preferred_element_type=F32)

    lane = lax.broadcasted_iota(jnp.int32, (blk, n_gate), 1)

    def body(jb, carry):
        m, l, acc = carry
        st = pl.multiple_of(jb * blk, blk)
        kj = k_ref[0, pl.ds(st, blk), :]
        vj = v_ref[0, pl.ds(st, blk), :]
        onehot = jnp.where(lane == hd * nb + jb, 1.0, 0.0).astype(BF16)
        ka = jnp.concatenate([kj, onehot], axis=1)
        sj = lax.dot_general(qa_ref[...], ka, _NT, preferred_element_type=F32)
        m_new = jnp.maximum(m, jnp.max(sj, axis=-1, keepdims=True))
        alpha = jnp.exp(m - m_new)
        pj = jnp.exp(sj - m_new)
        l = alpha * l + jnp.sum(pj, axis=-1, keepdims=True)
        acc = alpha * acc + jnp.dot(pj.astype(BF16), vj, preferred_element_type=F32)
        return m_new, l, acc

    m, l, acc = lax.fori_loop(0, i, body, (m0, l0, acc0))
    o_ref[0] = (acc / l).astype(o_ref.dtype)


def _attention(proj3, bias, nb, d_ssm):
    bsz, s, _ = proj3.shape
    n_gate = bias.shape[2]
    n_heads = n_gate // nb
    d_attn = n_heads * HEAD_DIM
    q0 = d_ssm // HEAD_DIM
    k0 = q0 + n_heads
    v0 = k0 + n_heads
    blk = MOBA_BLOCK
    return pl.pallas_call(
        functools.partial(_attn_kernel, nb=nb),
        grid=(bsz, n_heads, nb),
        in_specs=[
            pl.BlockSpec((1, blk, HEAD_DIM), lambda b, h, i: (b, i, q0 + h)),
            pl.BlockSpec((1, s, HEAD_DIM), lambda b, h, i: (b, 0, k0 + h)),
            pl.BlockSpec((1, s, HEAD_DIM), lambda b, h, i: (b, 0, v0 + h)),
            pl.BlockSpec((1, blk, n_gate), lambda b, h, i: (b, i, 0)),
        ],
        out_specs=pl.BlockSpec((1, blk, HEAD_DIM), lambda b, h, i: (b, i, h)),
        out_shape=jax.ShapeDtypeStruct((bsz, s, d_attn), BF16),
        scratch_shapes=[pltpu.VMEM((blk, HEAD_DIM + n_gate), BF16)],
        compiler_params=_params(3),
        name="moba_attn",
    )(proj3, proj3, proj3, bias)


def _ssm_tables(lam_re, lam_im, log_dt, b_re, b_im, c_re, c_im, d_skip):
    hi = lax.Precision.HIGHEST
    t_len = SSM_CHUNK
    n_groups, n_state = lam_re.shape
    dt = jnp.exp(log_dt)[:, None]
    ar = lam_re * dt
    ai = lam_im * dt
    steps = jnp.arange(t_len + 1, dtype=F32)[:, None, None]
    mag = jnp.exp(ar[None] * steps)
    pw_re = mag * jnp.cos(ai[None] * steps)
    pw_im = mag * jnp.sin(ai[None] * steps)
    e1 = jnp.expm1(ar)
    sh = jnp.sin(0.5 * ai)
    num_re = e1 * jnp.cos(ai) - 2.0 * sh * sh
    num_im = (e1 + 1.0) * jnp.sin(ai)
    den = lam_re * lam_re + lam_im * lam_im
    coef_re = (num_re * lam_re + num_im * lam_im) / den
    coef_im = (num_im * lam_re - num_re * lam_im) / den
    bb_re = coef_re[..., None] * b_re - coef_im[..., None] * b_im
    bb_im = coef_re[..., None] * b_im + coef_im[..., None] * b_re
    cp_re = c_re[None] * pw_re[:, :, None, :] - c_im[None] * pw_im[:, :, None, :]
    cp_im = c_re[None] * pw_im[:, :, None, :] + c_im[None] * pw_re[:, :, None, :]
    kern = (jnp.einsum('lgmp,gpn->lgmn', cp_re[:t_len], bb_re, precision=hi)
            - jnp.einsum('lgmp,gpn->lgmn', cp_im[:t_len], bb_im, precision=hi))
    s_ix = jnp.arange(t_len)[:, None]
    t_ix = jnp.arange(t_len)[None, :]
    lag = t_ix - s_ix
    toep = jnp.where((lag >= 0)[:, :, None, None, None], kern[jnp.clip(lag, 0, t_len - 1)], 0.0)
    toep = toep.transpose(2, 0, 4, 1, 3)
    eye_t = jnp.eye(t_len, dtype=F32)[None, :, None, :, None]
    eye_m = jnp.eye(SSM_GROUP, dtype=F32)[None, None, :, None, :]
    toep = toep + eye_t * eye_m * d_skip.reshape(n_groups, 1, 1, 1, SSM_GROUP)
    w = t_len * SSM_GROUP
    toep = toep.reshape(n_groups, w, w)
    rev_re = pw_re[:t_len][::-1]
    rev_im = pw_im[:t_len][::-1]
    pin_re = rev_re[:, :, None, :] * bb_re.transpose(0, 2, 1)[None] - rev_im[:, :, None, :] * bb_im.transpose(0, 2, 1)[None]
    pin_im = rev_re[:, :, None, :] * bb_im.transpose(0, 2, 1)[None] + rev_im[:, :, None, :] * bb_re.transpose(0, 2, 1)[None]
    pad_p = LANES - n_state
    pin = jnp.stack([pin_re, pin_im], axis=0)
    pin = pin.transpose(2, 0, 1, 3, 4).reshape(n_groups, 2, w, n_state)
    pin = jnp.pad(pin, ((0, 0), (0, 0), (0, 0), (0, pad_p)))
    q_re = cp_re[1:].transpose(1, 3, 0, 2).reshape(n_groups, n_state, w)
    q_im = -cp_im[1:].transpose(1, 3, 0, 2).reshape(n_groups, n_state, w)
    qout = jnp.pad(jnp.stack([q_re, q_im], axis=1), ((0, 0), (0, 0), (0, pad_p), (0, 0)))
    adec = jnp.stack([pw_re[t_len], pw_im[t_len]], axis=1)[:, :, None, :]
    adec = jnp.pad(adec, ((0, 0), (0, 0), (0, 0), (0, pad_p)))
    return toep.astype(BF16), pin.astype(BF16), qout.astype(BF16), adec


def _ssm_kernel(u_ref, toep_ref, pin_ref, qout_ref, adec_ref, o_ref, hre_ref, him_ref, *, bsz, nc):
    u = u_ref[0]
    hre_ref[...] = jnp.dot(u, pin_ref[0, 0], preferred_element_type=F32)
    him_ref[...] = jnp.dot(u, pin_ref[0, 1], preferred_element_type=F32)
    a_re = adec_ref[0, 0]
    a_im = adec_ref[0, 1]

    def step(c, carry):
        new = []
        for b in range(bsz):
            s_re, s_im = carry[b]
            r = b * nc + c
            loc_re = hre_ref[pl.ds(r, 1), :]
            loc_im = him_ref[pl.ds(r, 1), :]
            hre_ref[pl.ds(r, 1), :] = s_re
            him_ref[pl.ds(r, 1), :] = s_im
            new.append((a_re * s_re - a_im * s_im + loc_re,
                        a_re * s_im + a_im * s_re + loc_im))
        return tuple(new)

    zero = jnp.zeros((1, LANES), F32)
    lax.fori_loop(0, nc, step, tuple((zero, zero) for _ in range(bsz)), unroll=4)

    y = jnp.dot(u, toep_ref[0], preferred_element_type=F32)
    y = y + jnp.dot(hre_ref[...].astype(BF16), qout_ref[0, 0], preferred_element_type=F32)
    y = y + jnp.dot(him_ref[...].astype(BF16), qout_ref[0, 1], preferred_element_type=F32)
    o_ref[0] = jax.nn.gelu(y).astype(o_ref.dtype)


def _ssm(u_g, tables, bsz):
    toep, pin, qout, adec = tables
    n_groups, rows, w = u_g.shape
    nc = rows // bsz
    return pl.pallas_call(
        functools.partial(_ssm_kernel, bsz=bsz, nc=nc),
        grid=(n_groups,),
        in_specs=[
            pl.BlockSpec((1, rows, w), lambda g: (g, 0, 0)),
            pl.BlockSpec((1, w, w), lambda g: (g, 0, 0)),
            pl.BlockSpec((1, 2, w, LANES), lambda g: (g, 0, 0, 0)),
            pl.BlockSpec((1, 2, LANES, w), lambda g: (g, 0, 0, 0)),
            pl.BlockSpec((1, 2, 1, LANES), lambda g: (g, 0, 0, 0)),
        ],
        out_specs=pl.BlockSpec((1, rows, w), lambda g: (g, 0, 0)),
        out_shape=jax.ShapeDtypeStruct((n_groups, rows, w), BF16),
        scratch_shapes=[pltpu.VMEM((rows, LANES), F32), pltpu.VMEM((rows, LANES), F32)],
        compiler_params=_params(1),
        name="s5_chunked",
    )(u_g, toep, pin, qout, adec)


def _merge_kernel(y_ref, att_ref, ga_ref, gb_ref, x_ref, wglu_ref, wus_ref, wua_ref, wout_ref,
                  g_ref, o_ref):
    y = y_ref[...]
    z = jnp.dot(y, wglu_ref[...], preferred_element_type=F32)
    s5 = (y.astype(F32) * jax.nn.sigmoid(z)).astype(BF16)
    ys = jnp.dot(s5, wus_ref[...], preferred_element_type=F32)
    ya = jnp.dot(att_ref[...], wua_ref[...], preferred_element_type=F32)
    m = (jax.nn.sigmoid(ga_ref[...].astype(F32)) * ys
         + jax.nn.sigmoid(gb_ref[...].astype(F32)) * ya)
    o = jnp.dot(m.astype(BF16), wout_ref[...], preferred_element_type=F32)
    o_ref[...] = x_ref[...] + _rmsnorm_rows(o, g_ref[...])


def _merge(y2, att2, proj2, x2, w_glu, w_us, w_ua, w_out, g_post):
    n_rows, d = x2.shape
    d_ssm = y2.shape[1]
    d_attn = att2.shape[1]
    tm = min(ROW_TILE_MERGE, n_rows)
    ga_blk = (d_ssm + 3 * d_attn) // d
    const = lambda i: (0, 0)
    one = pl.Buffered(1)
    return pl.pallas_call(
        _merge_kernel,
        grid=(n_rows // tm,),
        in_specs=[
            pl.BlockSpec((tm, d_ssm), lambda i: (i, 0)),
            pl.BlockSpec((tm, d_attn), lambda i: (i, 0)),
            pl.BlockSpec((tm, d), lambda i: (i, ga_blk)),
            pl.BlockSpec((tm, d), lambda i: (i, ga_blk + 1)),
            pl.BlockSpec((tm, d), lambda i: (i, 0)),
            pl.BlockSpec(w_glu.shape, const, pipeline_mode=one),
            pl.BlockSpec(w_us.shape, const, pipeline_mode=one),
            pl.BlockSpec(w_ua.shape, const, pipeline_mode=one),
            pl.BlockSpec(w_out.shape, const, pipeline_mode=one),
            pl.BlockSpec((1, d), const),
        ],
        out_specs=pl.BlockSpec((tm, d), lambda i: (i, 0)),
        out_shape=jax.ShapeDtypeStruct((n_rows, d), F32),
        compiler_params=_params(1),
        name="merge",
    )(y2, att2, proj2, proj2, x2, w_glu, w_us, w_ua, w_out, g_post)


def _ffn_kernel(x_ref, gpre_ref, wa_ref, wv_ref, cwa_ref, cwv_ref, cba_ref, cbv_ref, wd_ref,
                gpost_ref, o_ref, h_ref, acc_ref, halo_a_ref, halo_v_ref, *, tiles_per_seq):
    i = pl.program_id(0)
    c = pl.program_id(1)

    @pl.when(c == 0)
    def _():
        _rmsnorm_into(h_ref, x_ref, gpre_ref)
        acc_ref[...] = jnp.zeros_like(acc_ref)

    first = (i % tiles_per_seq) == 0

    def conv(w_ref, cw_ref, cb_ref, halo_ref):
        z = jnp.dot(h_ref[...], w_ref[...], preferred_element_type=F32)
        tm = z.shape[0]
        @pl.when(first)
        def _():
            halo_ref[c] = jnp.zeros(halo_ref.shape[1:], F32)

        prev = halo_ref[c]
        halo_ref[c] = z[tm - SUBLANES:, :]
        row = lax.broadcasted_iota(jnp.int32, z.shape, 0)
        p1 = prev[SUBLANES - 1:SUBLANES, :]
        p2 = prev[SUBLANES - 2:SUBLANES - 1, :]
        z1 = jnp.where(row == 0, p1, pltpu.roll(z, 1, 0))
        z2 = jnp.where(row == 0, p2, jnp.where(row == 1, p1, pltpu.roll(z, 2, 0)))
        cw = cw_ref[...]
        return cw[0:1, :] * z2 + cw[1:2, :] * z1 + cw[2:3, :] * z + cb_ref[...]

    a = conv(wa_ref, cwa_ref, cba_ref, halo_a_ref)
    v = conv(wv_ref, cwv_ref, cbv_ref, halo_v_ref)
    gated = (a * jax.nn.sigmoid(a) * v).astype(BF16)
    acc_ref[...] += jnp.dot(gated, wd_ref[...], preferred_element_type=F32)

    @pl.when(c == pl.num_programs(1) - 1)
    def _():
        o_ref[...] = x_ref[...] + _rmsnorm_rows(acc_ref[...], gpost_ref[...])


def _ffn(x2, g_pre, w_up, conv_w, conv_b, w_down, g_post, seq_len):
    n_rows, d = x2.shape
    d_ff = w_down.shape[0]
    tm = min(ROW_TILE_FFN, seq_len)
    tf = COL_TILE_FFN
    nff = d_ff // tf
    const = lambda i, c: (0, 0)
    return pl.pallas_call(
        functools.partial(_ffn_kernel, tiles_per_seq=seq_len // tm),
        grid=(n_rows // tm, nff),
        in_specs=[
            pl.BlockSpec((tm, d), lambda i, c: (i, 0)),
            pl.BlockSpec((1, d), const),
            pl.BlockSpec((d, tf), lambda i, c: (0, c)),
            pl.BlockSpec((d, tf), lambda i, c: (0, nff + c)),
            pl.BlockSpec((CONV_W, tf), lambda i, c: (0, c)),
            pl.BlockSpec((CONV_W, tf), lambda i, c: (0, nff + c)),
            pl.BlockSpec((1, tf), lambda i, c: (0, c)),
            pl.BlockSpec((1, tf), lambda i, c: (0, nff + c)),
            pl.BlockSpec((tf, d), lambda i, c: (c, 0)),
            pl.BlockSpec((1, d), const),
        ],
        out_specs=pl.BlockSpec((tm, d), lambda i, c: (i, 0)),
        out_shape=jax.ShapeDtypeStruct((n_rows, d), F32),
        scratch_shapes=[
            pltpu.VMEM((tm, d), BF16),
            pltpu.VMEM((tm, d), F32),
            pltpu.VMEM((nff, SUBLANES, tf), F32),
            pltpu.VMEM((nff, SUBLANES, tf), F32),
        ],
        compiler_params=_params(2),
        name="convglu_ffn",
    )(x2, g_pre, w_up, w_up, conv_w, conv_w, conv_b, conv_b, w_down, g_post)


def _layer(x2, bsz, seq_len, g_pre_mix, w_in, lam_re, lam_im, log_dt, b_re, b_im, c_re, c_im,
           d_skip, w_glu, w_up_ssm, w_up_attn, w_out, g_post_mix, g_pre_ffn, w_ffn_up, conv_w,
           conv_b, w_ffn_down, g_post_ffn):
    n_rows, d = x2.shape
    d_ssm = w_glu.shape[0]
    d_attn = w_up_attn.shape[0]
    n_heads = d_attn // HEAD_DIM
    n_groups = d_ssm // SSM_GROUP
    nb = seq_len // MOBA_BLOCK
    row = lambda v: v.reshape(1, -1)

    col_scale = jnp.ones((w_in.shape[1],), F32).at[d_ssm:d_ssm + d_attn].set(1.0 / math.sqrt(HEAD_DIM))
    proj2 = _inproj(x2, row(g_pre_mix), w_in.astype(BF16), row(col_scale))

    hbar = _blockmean(x2, row(g_pre_mix)).reshape(bsz, 1, nb, d)
    hbar_t = jnp.broadcast_to(hbar, (bsz, n_heads, nb, d)).reshape(bsz * n_heads * nb, d)
    kbd = _kmean(hbar_t, w_in, nb, d_ssm, d_attn).reshape(bsz, n_heads * nb, d_attn)
    z = _gatevec(w_in, kbd, d_ssm)
    bias = _select(x2.reshape(bsz, seq_len, d), row(g_pre_mix), z, nb)

    att = _attention(proj2.reshape(bsz, seq_len, -1), bias, nb, d_ssm)

    nc = seq_len // SSM_CHUNK
    u_g = proj2[:, :d_ssm].reshape(bsz, nc, SSM_CHUNK, n_groups, SSM_GROUP)
    u_g = u_g.transpose(3, 0, 1, 2, 4).reshape(n_groups, bsz * nc, SSM_CHUNK * SSM_GROUP)
    tables = _ssm_tables(lam_re, lam_im, log_dt, b_re, b_im, c_re, c_im, d_skip)
    y_g = _ssm(u_g, tables, bsz)
    y2 = y_g.reshape(n_groups, bsz, nc, SSM_CHUNK, SSM_GROUP).transpose(1, 2, 3, 0, 4)
    y2 = y2.reshape(n_rows, d_ssm)

    x2 = _merge(y2, att.reshape(n_rows, d_attn), proj2, x2, w_glu.astype(BF16),
                w_up_ssm.astype(BF16), w_up_attn.astype(BF16), w_out.astype(BF16), row(g_post_mix))

    return _ffn(x2, row(g_pre_ffn), w_ffn_up.astype(BF16), conv_w, row(conv_b),
                w_ffn_down.astype(BF16), row(g_post_ffn), seq_len)


def kernel(x, g_pre_mix, w_in, lam_re, lam_im, log_dt, b_re, b_im, c_re, c_im, d_skip, w_glu,
           w_up_ssm, w_up_attn, w_out, g_post_mix, g_pre_ffn, w_ffn_up, conv_w, conv_b,
           w_ffn_down, g_post_ffn):
    bsz, seq_len, d = x.shape
    assert seq_len % MOBA_BLOCK == 0 and seq_len % SSM_CHUNK == 0
    x2 = x.reshape(bsz * seq_len, d)
    for l in range(w_in.shape[0]):
        x2 = _layer(x2, bsz, seq_len, g_pre_mix[l], w_in[l], lam_re[l], lam_im[l], log_dt[l],
                    b_re[l], b_im[l], c_re[l], c_im[l], d_skip[l], w_glu[l], w_up_ssm[l],
                    w_up_attn[l], w_out[l], g_post_mix[l], g_pre_ffn[l], w_ffn_up[l], conv_w[l],
                    conv_b[l], w_ffn_down[l], g_post_ffn[l])
    return x2.reshape(bsz, seq_len, d)
```

```python
import functools
import math

import jax
import jax.numpy as jnp
from jax import lax
from jax.experimental import pallas as pl
from jax.experimental.pallas import tpu as pltpu

F32 = jnp.float32
BF16 = jnp.bfloat16

EPS = 1e-6
NEG = -1e30
SSM_GROUP = 16
HEAD_DIM = 128
MOBA_BLOCK = 256
MOBA_TOPK = 3
CONV_W = 3

LANES = 128
SUBLANES = 8
VMEM_LIMIT_BYTES = 56 * 1024 * 1024

SSM_CHUNK = 32
ROW_TILE_PROJ = 1024
COL_TILE_PROJ = 512
ROW_TILE_GATE = 512
ROW_TILE_MERGE = 512
ROW_TILE_FFN = 512
COL_TILE_FFN = 512
COL_SUB_FFN = 256
K_TILE_GATE = 512
NORM_ROWS = 128
ATTN_Q_BLOCKS = 2

_NT = (((1,), (1,)), ((), ()))


def _params(n_axes):
    return pltpu.CompilerParams(
        dimension_semantics=("arbitrary",) * n_axes,
        vmem_limit_bytes=VMEM_LIMIT_BYTES,
    )


def _rmsnorm_rows(x, g):
    ms = jnp.mean(x * x, axis=-1, keepdims=True)
    return x * lax.rsqrt(ms + EPS) * g


def _rmsnorm_into(h_ref, x_ref, g_ref):
    step_rows = min(NORM_ROWS, x_ref.shape[0])

    def step(r, carry):
        r0 = pl.multiple_of(r * step_rows, step_rows)
        h_ref[pl.ds(r0, step_rows), :] = _rmsnorm_rows(
            x_ref[pl.ds(r0, step_rows), :], g_ref[...]).astype(h_ref.dtype)
        return carry

    lax.fori_loop(0, x_ref.shape[0] // step_rows, step, 0)


def _inproj_kernel(x_ref, g_ref, w_ref, cs_ref, o_ref, h_ref):
    @pl.when(pl.program_id(1) == 0)
    def _():
        _rmsnorm_into(h_ref, x_ref, g_ref)

    acc = jnp.dot(h_ref[...], w_ref[...], preferred_element_type=F32)
    o_ref[...] = (acc * cs_ref[...]).astype(o_ref.dtype)


def _inproj(x2, g, w_bf, layer, col_scale):
    n_rows, d = x2.shape
    d_in = w_bf.shape[2]
    tm = min(ROW_TILE_PROJ, n_rows)
    tn = COL_TILE_PROJ
    return pl.pallas_call(
        _inproj_kernel,
        grid=(n_rows // tm, d_in // tn),
        in_specs=[
            pl.BlockSpec((tm, d), lambda i, j: (i, 0)),
            pl.BlockSpec((1, d), lambda i, j: (0, 0)),
            pl.BlockSpec((None, d, tn), lambda i, j: (layer, 0, j)),
            pl.BlockSpec((1, tn), lambda i, j: (0, j)),
        ],
        out_specs=pl.BlockSpec((tm, tn), lambda i, j: (i, j)),
        out_shape=jax.ShapeDtypeStruct((n_rows, d_in), BF16),
        scratch_shapes=[pltpu.VMEM((tm, d), BF16)],
        compiler_params=_params(2),
        name="inproj",
    )(x2, g, w_bf, col_scale)


def _blockmean_kernel(x_ref, g_ref, o_ref):
    h = _rmsnorm_rows(x_ref[...], g_ref[...])
    o_ref[0] = jnp.mean(h, axis=0, keepdims=True)


def _blockmean(x2, g):
    n_rows, d = x2.shape
    n_blocks = n_rows // MOBA_BLOCK
    return pl.pallas_call(
        _blockmean_kernel,
        grid=(n_blocks,),
        in_specs=[
            pl.BlockSpec((MOBA_BLOCK, d), lambda i: (i, 0)),
            pl.BlockSpec((1, d), lambda i: (0, 0)),
        ],
        out_specs=pl.BlockSpec((1, 1, d), lambda i: (i, 0, 0)),
        out_shape=jax.ShapeDtypeStruct((n_blocks, 1, d), F32),
        compiler_params=_params(1),
        name="blockmean",
    )(x2, g)


def _kmean_kernel(hb_ref, wk_ref, o_ref, *, nb):
    @pl.when(pl.program_id(0) == 0)
    def _():
        o_ref[...] = jnp.zeros_like(o_ref)

    o_ref[...] += jnp.dot(hb_ref[...], wk_ref[...], preferred_element_type=F32,
                          precision=lax.Precision.HIGHEST)

    @pl.when(pl.program_id(0) == pl.num_programs(0) - 1)
    def _():
        r = lax.broadcasted_iota(jnp.int32, o_ref.shape, 0)
        c = lax.broadcasted_iota(jnp.int32, o_ref.shape, 1)
        n_gate = (o_ref.shape[1] // HEAD_DIM) * nb
        keep = (c // HEAD_DIM) == ((r % n_gate) // nb)
        o_ref[...] = jnp.where(keep, o_ref[...], 0.0)


def _kmean(hbar_t, w_in, layer, nb, d_ssm, d_attn):
    rows, d = hbar_t.shape
    k_col_block = (d_ssm + d_attn) // d_attn
    tk = K_TILE_GATE
    return pl.pallas_call(
        functools.partial(_kmean_kernel, nb=nb),
        grid=(d // tk,),
        in_specs=[
            pl.BlockSpec((rows, tk), lambda kk: (0, kk)),
            pl.BlockSpec((None, tk, d_attn), lambda kk: (layer, kk, k_col_block)),
        ],
        out_specs=pl.BlockSpec((rows, d_attn), lambda kk: (0, 0)),
        out_shape=jax.ShapeDtypeStruct((rows, d_attn), F32),
        compiler_params=_params(1),
        name="kmean",
    )(hbar_t, w_in)


def _gatevec_kernel(wq_ref, kbd_ref, o_ref):
    o_ref[0] = lax.dot_general(wq_ref[...], kbd_ref[0], _NT, preferred_element_type=F32,
                               precision=lax.Precision.HIGHEST)


def _gatevec(w_in, layer, kbd, d_ssm):
    bsz, n_gate, d_attn = kbd.shape
    d = w_in.shape[1]
    q_col_block = d_ssm // d_attn
    tk = K_TILE_GATE
    return pl.pallas_call(
        _gatevec_kernel,
        grid=(bsz, d // tk),
        in_specs=[
            pl.BlockSpec((None, tk, d_attn), lambda b, kk: (layer, kk, q_col_block)),
            pl.BlockSpec((1, n_gate, d_attn), lambda b, kk: (b, 0, 0)),
        ],
        out_specs=pl.BlockSpec((1, tk, n_gate), lambda b, kk: (b, kk, 0)),
        out_shape=jax.ShapeDtypeStruct((bsz, d, n_gate), F32),
        compiler_params=_params(2),
        name="gatevec",
    )(w_in, kbd)


def _select_kernel(x_ref, g_ref, z_ref, o_ref, *, nb, tq):
    h = _rmsnorm_rows(x_ref[0], g_ref[...])
    gate = jnp.dot(h, z_ref[0], preferred_element_type=F32, precision=lax.Precision.HIGHEST)
    n_gate = gate.shape[1]
    row = lax.broadcasted_iota(jnp.int32, gate.shape, 0) + pl.program_id(1) * tq
    qblk = row // MOBA_BLOCK
    j = lax.broadcasted_iota(jnp.int32, gate.shape, 1) % nb
    past = j < qblk
    gate = jnp.where(past, gate, NEG)
    cnt = jnp.zeros(gate.shape, F32)
    for r in range(1, nb):
        lower = pltpu.roll(gate, r, 1)
        upper = pltpu.roll(gate, (r - nb) % n_gate, 1)
        has_lower = j >= r
        other = jnp.where(has_lower, lower, upper)
        wins_tie = jnp.where(other >= gate, 1.0, 0.0)
        wins_strict = jnp.where(other > gate, 1.0, 0.0)
        cnt = cnt + jnp.where(has_lower, wins_tie, wins_strict)
    keep = jnp.logical_or(jnp.logical_and(past, cnt < float(MOBA_TOPK)), j == qblk)
    o_ref[0] = jnp.where(keep, 0.0, NEG)


def _select(x3, g, z, nb):
    bsz, s, d = x3.shape
    n_gate = z.shape[2]
    tq = min(ROW_TILE_GATE, s)
    return pl.pallas_call(
        functools.partial(_select_kernel, nb=nb, tq=tq),
        grid=(bsz, s // tq),
        in_specs=[
            pl.BlockSpec((1, tq, d), lambda b, i: (b, i, 0)),
            pl.BlockSpec((1, d), lambda b, i: (0, 0)),
            pl.BlockSpec((1, d, n_gate), lambda b, i: (b, 0, 0)),
        ],
        out_specs=pl.BlockSpec((1, tq, n_gate), lambda b, i: (b, i, 0)),
        out_shape=jax.ShapeDtypeStruct((bsz, s, n_gate), F32),
        compiler_params=_params(2),
        name="select",
    )(x3, g, z)


def _attn_kernel(q_ref, k_ref, v_ref, bias_ref, o_ref, qa_ref, s_ref, m_ref, l_ref, acc_ref, *, nb):
    hd = pl.program_id(1)
    i = pl.program_id(2)
    blk = MOBA_BLOCK
    tq = q_ref.shape[1]
    n_gate = bias_ref.shape[2]
    qa_ref[:, :HEAD_DIM] = q_ref[0]
    qa_ref[:, HEAD_DIM:] = bias_ref[0].astype(BF16)
    m_ref[...] = jnp.full(m_ref.shape, -jnp.inf, F32)
    l_ref[...] = jnp.zeros(l_ref.shape, F32)
    acc_ref[...] = jnp.zeros(acc_ref.shape, F32)
    lane = lax.broadcasted_iota(jnp.int32, (blk, n_gate), 1)

    def scores(slot, jb):
        st = pl.multiple_of(jb * blk, blk)
        onehot = jnp.where(lane == hd * nb + jb, 1.0, 0.0).astype(BF16)
        ka = jnp.concatenate([k_ref[0, pl.ds(st, blk), :], onehot], axis=1)
        s_ref[slot] = lax.dot_general(qa_ref[...], ka, _NT, preferred_element_type=F32)

    def update(slot, jb, causal):
        s = s_ref[slot]
        if causal:
            qpos = lax.broadcasted_iota(jnp.int32, s.shape, 0) + i * tq
            kpos = lax.broadcasted_iota(jnp.int32, s.shape, 1) + jb * blk
            s = jnp.where(kpos <= qpos, s, NEG)
        st = pl.multiple_of(jb * blk, blk)
        m_old = m_ref[...]
        m_new = jnp.maximum(m_old, jnp.max(s, axis=-1, keepdims=True))
        alpha = jnp.exp(m_old - m_new)
        p = jnp.exp(s - m_new)
        l_ref[...] = alpha * l_ref[...] + jnp.sum(p, axis=-1, keepdims=True)
        acc_ref[...] = alpha * acc_ref[...] + jnp.dot(
            p.astype(BF16), v_ref[0, pl.ds(st, blk), :], preferred_element_type=F32)
        m_ref[...] = m_new

    scores(0, 0)

    def pair(jj, carry):
        scores(1, 2 * jj + 1)
        update(0, 2 * jj, False)
        scores(0, 2 * jj + 2)
        update(1, 2 * jj + 1, False)
        return carry

    lax.fori_loop(0, i, pair, 0)
    scores(1, 2 * i + 1)
    update(0, 2 * i, True)
    update(1, 2 * i + 1, True)
    o_ref[0] = (acc_ref[...] / l_ref[...]).astype(o_ref.dtype)


def _attention(proj3, bias, nb, d_ssm):
    bsz, s, _ = proj3.shape
    n_gate = bias.shape[2]
    n_heads = n_gate // nb
    d_attn = n_heads * HEAD_DIM
    q0 = d_ssm // HEAD_DIM
    k0 = q0 + n_heads
    v0 = k0 + n_heads
    assert ATTN_Q_BLOCKS == 2 and nb % ATTN_Q_BLOCKS == 0
    tq = ATTN_Q_BLOCKS * MOBA_BLOCK
    return pl.pallas_call(
        functools.partial(_attn_kernel, nb=nb),
        grid=(bsz, n_heads, s // tq),
        in_specs=[
            pl.BlockSpec((1, tq, HEAD_DIM), lambda b, h, i: (b, i, q0 + h)),
            pl.BlockSpec((1, s, HEAD_DIM), lambda b, h, i: (b, 0, k0 + h)),
            pl.BlockSpec((1, s, HEAD_DIM), lambda b, h, i: (b, 0, v0 + h)),
            pl.BlockSpec((1, tq, n_gate), lambda b, h, i: (b, i, 0)),
        ],
        out_specs=pl.BlockSpec((1, tq, HEAD_DIM), lambda b, h, i: (b, i, h)),
        out_shape=jax.ShapeDtypeStruct((bsz, s, d_attn), BF16),
        scratch_shapes=[
            pltpu.VMEM((tq, HEAD_DIM + n_gate), BF16),
            pltpu.VMEM((2, tq, MOBA_BLOCK), F32),
            pltpu.VMEM((tq, 1), F32),
            pltpu.VMEM((tq, 1), F32),
            pltpu.VMEM((tq, HEAD_DIM), F32),
        ],
        compiler_params=_params(3),
        name="moba_attn",
    )(proj3, proj3, proj3, bias)


def _ssm_tables(lam_re, lam_im, log_dt, b_re, b_im, c_re, c_im, d_skip):
    hi = lax.Precision.HIGHEST
    t_len = SSM_CHUNK
    n_groups, n_state = lam_re.shape
    n_ch = SSM_GROUP
    w = t_len * n_ch
    dt = jnp.exp(log_dt)[:, None]
    ar = lam_re * dt
    ai = lam_im * dt
    steps = jnp.arange(t_len + 1, dtype=F32)[None, :, None]
    mag = jnp.exp(ar[:, None, :] * steps)
    pw_re = mag * jnp.cos(ai[:, None, :] * steps)
    pw_im = mag * jnp.sin(ai[:, None, :] * steps)
    e1 = jnp.expm1(ar)
    sh = jnp.sin(0.5 * ai)
    num_re = e1 * jnp.cos(ai) - 2.0 * sh * sh
    num_im = (e1 + 1.0) * jnp.sin(ai)
    den = lam_re * lam_re + lam_im * lam_im
    coef_re = (num_re * lam_re + num_im * lam_im) / den
    coef_im = (num_im * lam_re - num_re * lam_im) / den
    bb_re = coef_re[..., None] * b_re - coef_im[..., None] * b_im
    bb_im = coef_re[..., None] * b_im + coef_im[..., None] * b_re
    cp_re = c_re[:, None] * pw_re[:, :t_len, None, :] - c_im[:, None] * pw_im[:, :t_len, None, :]
    cp_im = c_re[:, None] * pw_im[:, :t_len, None, :] + c_im[:, None] * pw_re[:, :t_len, None, :]
    kern = (jnp.einsum('glmp,gpn->gnlm', cp_re, bb_re, precision=hi)
            - jnp.einsum('glmp,gpn->gnlm', cp_im, bb_im, precision=hi))
    toep = jnp.stack(
        [jnp.pad(kern[:, :, :t_len - s, :], ((0, 0), (0, 0), (s, 0), (0, 0))) for s in range(t_len)],
        axis=1)
    eye_t = jnp.eye(t_len, dtype=F32)[None, :, None, :, None]
    eye_m = jnp.eye(n_ch, dtype=F32)[None, None, :, None, :]
    toep = toep + eye_t * eye_m * d_skip.reshape(n_groups, 1, 1, 1, n_ch)
    toep = toep.reshape(n_groups, w, w)
    rev_re = pw_re[:, :t_len][:, ::-1][:, :, None, :]
    rev_im = pw_im[:, :t_len][:, ::-1][:, :, None, :]
    bt_re = bb_re.transpose(0, 2, 1)[:, None]
    bt_im = bb_im.transpose(0, 2, 1)[:, None]
    pin = jnp.stack([rev_re * bt_re - rev_im * bt_im, rev_re * bt_im + rev_im * bt_re], axis=1)
    pad_p = LANES - n_state
    pin = jnp.pad(pin.reshape(n_groups, 2, w, n_state), ((0, 0), (0, 0), (0, 0), (0, pad_p)))
    pt_re = pw_re.transpose(0, 2, 1)[:, :, 1:, None]
    pt_im = pw_im.transpose(0, 2, 1)[:, :, 1:, None]
    ct_re = c_re.transpose(0, 2, 1)[:, :, None, :]
    ct_im = c_im.transpose(0, 2, 1)[:, :, None, :]
    qout = jnp.stack([ct_re * pt_re - ct_im * pt_im, -(ct_re * pt_im + ct_im * pt_re)], axis=1)
    qout = jnp.pad(qout.reshape(n_groups, 2, n_state, w), ((0, 0), (0, 0), (0, pad_p), (0, 0)))
    adec = jnp.stack([pw_re[:, t_len], pw_im[:, t_len]], axis=1)[:, :, None, :]
    adec = jnp.pad(adec, ((0, 0), (0, 0), (0, 0), (0, pad_p)))
    return toep.astype(BF16), pin.astype(BF16), qout.astype(BF16), adec


def _ssm_kernel(u_ref, toep_ref, pin_ref, qout_ref, adec_ref, o_ref, hre_ref, him_ref, *, bsz, nc):
    u = u_ref[0]
    hre_ref[...] = jnp.dot(u, pin_ref[0, 0], preferred_element_type=F32)
    him_ref[...] = jnp.dot(u, pin_ref[0, 1], preferred_element_type=F32)
    a_re = adec_ref[0, 0]
    a_im = adec_ref[0, 1]

    def step(c, carry):
        new = []
        for b in range(bsz):
            s_re, s_im = carry[b]
            r = b * nc + c
            loc_re = hre_ref[pl.ds(r, 1), :]
            loc_im = him_ref[pl.ds(r, 1), :]
            hre_ref[pl.ds(r, 1), :] = s_re
            him_ref[pl.ds(r, 1), :] = s_im
            new.append((a_re * s_re - a_im * s_im + loc_re,
                        a_re * s_im + a_im * s_re + loc_im))
        return tuple(new)

    zero = jnp.zeros((1, LANES), F32)
    lax.fori_loop(0, nc, step, tuple((zero, zero) for _ in range(bsz)), unroll=4)

    y = jnp.dot(u, toep_ref[0], preferred_element_type=F32)
    y = y + jnp.dot(hre_ref[...].astype(BF16), qout_ref[0, 0], preferred_element_type=F32)
    y = y + jnp.dot(him_ref[...].astype(BF16), qout_ref[0, 1], preferred_element_type=F32)
    o_ref[0] = jax.nn.gelu(y).astype(o_ref.dtype)


def _ssm(u_g, tables, layer, bsz):
    toep, pin, qout, adec = tables
    n_groups, rows, w = u_g.shape
    nc = rows // bsz
    return pl.pallas_call(
        functools.partial(_ssm_kernel, bsz=bsz, nc=nc),
        grid=(n_groups,),
        in_specs=[
            pl.BlockSpec((1, rows, w), lambda g: (g, 0, 0)),
            pl.BlockSpec((None, 1, w, w), lambda g: (layer, g, 0, 0)),
            pl.BlockSpec((None, 1, 2, w, LANES), lambda g: (layer, g, 0, 0, 0)),
            pl.BlockSpec((None, 1, 2, LANES, w), lambda g: (layer, g, 0, 0, 0)),
            pl.BlockSpec((None, 1, 2, 1, LANES), lambda g: (layer, g, 0, 0, 0)),
        ],
        out_specs=pl.BlockSpec((1, rows, w), lambda g: (g, 0, 0)),
        out_shape=jax.ShapeDtypeStruct((n_groups, rows, w), BF16),
        scratch_shapes=[pltpu.VMEM((rows, LANES), F32), pltpu.VMEM((rows, LANES), F32)],
        compiler_params=_params(1),
        name="s5_chunked",
    )(u_g, toep, pin, qout, adec)


def _merge_kernel(y_ref, att_ref, ga_ref, gb_ref, x_ref, wglu_ref, wus_ref, wua_ref, wout_ref,
                  g_ref, o_ref):
    y = y_ref[...]
    z = jnp.dot(y, wglu_ref[...], preferred_element_type=F32)
    s5 = (y.astype(F32) * jax.nn.sigmoid(z)).astype(BF16)
    ys = jnp.dot(s5, wus_ref[...], preferred_element_type=F32)
    ya = jnp.dot(att_ref[...], wua_ref[...], preferred_element_type=F32)
    m = (jax.nn.sigmoid(ga_ref[...].astype(F32)) * ys
         + jax.nn.sigmoid(gb_ref[...].astype(F32)) * ya)
    o = jnp.dot(m.astype(BF16), wout_ref[...], preferred_element_type=F32)
    o_ref[...] = x_ref[...] + _rmsnorm_rows(o, g_ref[...])


def _merge(y2, att2, proj2, x2, w_glu, w_us, w_ua, w_out, layer, g_post):
    n_rows, d = x2.shape
    d_ssm = y2.shape[1]
    d_attn = att2.shape[1]
    tm = min(ROW_TILE_MERGE, n_rows)
    ga_blk = (d_ssm + 3 * d_attn) // d

    def weight(wt):
        return pl.BlockSpec((None,) + wt.shape[1:], lambda i: (layer, 0, 0),
                            pipeline_mode=pl.Buffered(1))

    return pl.pallas_call(
        _merge_kernel,
        grid=(n_rows // tm,),
        in_specs=[
            pl.BlockSpec((tm, d_ssm), lambda i: (i, 0)),
            pl.BlockSpec((tm, d_attn), lambda i: (i, 0)),
            pl.BlockSpec((tm, d), lambda i: (i, ga_blk)),
            pl.BlockSpec((tm, d), lambda i: (i, ga_blk + 1)),
            pl.BlockSpec((tm, d), lambda i: (i, 0)),
            weight(w_glu), weight(w_us), weight(w_ua), weight(w_out),
            pl.BlockSpec((1, d), lambda i: (0, 0)),
        ],
        out_specs=pl.BlockSpec((tm, d), lambda i: (i, 0)),
        out_shape=jax.ShapeDtypeStruct((n_rows, d), F32),
        compiler_params=_params(1),
        name="merge",
    )(y2, att2, proj2, proj2, x2, w_glu, w_us, w_ua, w_out, g_post)


def _ffn_kernel(x_ref, gpre_ref, wa_ref, wv_ref, cwa_ref, cwv_ref, cba_ref, cbv_ref, wd_ref,
                gpost_ref, o_ref, h_ref, acc_ref, halo_a_ref, halo_v_ref, ca_ref, cv_ref,
                *, tiles_per_seq):
    i = pl.program_id(0)
    c = pl.program_id(1)
    tm = x_ref.shape[0]
    n_sub = wa_ref.shape[1] // COL_SUB_FFN

    @pl.when(c == 0)
    def _():
        _rmsnorm_into(h_ref, x_ref, gpre_ref)
        acc_ref[...] = jnp.zeros_like(acc_ref)

    @pl.when((i % tiles_per_seq) == 0)
    def _():
        zeros = jnp.zeros(halo_a_ref.shape[1:], F32)
        for sub in range(n_sub):
            halo_a_ref[c * n_sub + sub] = zeros
            halo_v_ref[c * n_sub + sub] = zeros

    h = h_ref[...]
    cols = [slice(sub * COL_SUB_FFN, (sub + 1) * COL_SUB_FFN) for sub in range(n_sub)]
    za = [jnp.dot(h, wa_ref[:, cs], preferred_element_type=F32) for cs in cols]
    zv = [jnp.dot(h, wv_ref[:, cs], preferred_element_type=F32) for cs in cols]

    row8 = lax.broadcasted_iota(jnp.int32, (SUBLANES, COL_SUB_FFN), 0)

    def conv(z, cw, cb, halo_ref, slot, out_ref):
        prev = halo_ref[slot]
        halo_ref[slot] = z[tm - SUBLANES:, :]
        w0, w1, w2 = cw[0:1, :], cw[1:2, :], cw[2:3, :]
        out_ref[...] = w0 * pltpu.roll(z, 2, 0) + w1 * pltpu.roll(z, 1, 0) + w2 * z + cb
        top = z[:SUBLANES, :]
        p1 = prev[SUBLANES - 1:SUBLANES, :]
        p2 = prev[SUBLANES - 2:SUBLANES - 1, :]
        t1 = jnp.where(row8 == 0, p1, pltpu.roll(top, 1, 0))
        t2 = jnp.where(row8 == 0, p2, jnp.where(row8 == 1, p1, pltpu.roll(top, 2, 0)))
        out_ref[0:SUBLANES, :] = w0 * t2 + w1 * t1 + w2 * top + cb

    for sub, cs in enumerate(cols):
        slot = c * n_sub + sub
        conv(za[sub], cwa_ref[:, cs], cba_ref[:, cs], halo_a_ref, slot, ca_ref.at[sub])
        conv(zv[sub], cwv_ref[:, cs], cbv_ref[:, cs], halo_v_ref, slot, cv_ref.at[sub])
        a = ca_ref[sub]
        gated = (a * jax.nn.sigmoid(a) * cv_ref[sub]).astype(BF16)
        acc_ref[...] += jnp.dot(gated, wd_ref[cs, :], preferred_element_type=F32)

    @pl.when(c == pl.num_programs(1) - 1)
    def _():
        o_ref[...] = x_ref[...] + _rmsnorm_rows(acc_ref[...], gpost_ref[...])


def _ffn(x2, g_pre, w_up, conv_w, conv_b, w_down, layer, g_post, seq_len):
    n_rows, d = x2.shape
    d_ff = w_down.shape[1]
    tm = min(ROW_TILE_FFN, seq_len)
    tf = COL_TILE_FFN
    nff = d_ff // tf
    n_sub = tf // COL_SUB_FFN
    const = lambda i, c: (0, 0)
    return pl.pallas_call(
        functools.partial(_ffn_kernel, tiles_per_seq=seq_len // tm),
        grid=(n_rows // tm, nff),
        in_specs=[
            pl.BlockSpec((tm, d), lambda i, c: (i, 0)),
            pl.BlockSpec((1, d), const),
            pl.BlockSpec((None, d, tf), lambda i, c: (layer, 0, c)),
            pl.BlockSpec((None, d, tf), lambda i, c: (layer, 0, nff + c)),
            pl.BlockSpec((None, CONV_W, tf), lambda i, c: (layer, 0, c)),
            pl.BlockSpec((None, CONV_W, tf), lambda i, c: (layer, 0, nff + c)),
            pl.BlockSpec((None, 1, tf), lambda i, c: (layer, 0, c)),
            pl.BlockSpec((None, 1, tf), lambda i, c: (layer, 0, nff + c)),
            pl.BlockSpec((None, tf, d), lambda i, c: (layer, c, 0)),
            pl.BlockSpec((1, d), const),
        ],
        out_specs=pl.BlockSpec((tm, d), lambda i, c: (i, 0)),
        out_shape=jax.ShapeDtypeStruct((n_rows, d), F32),
        scratch_shapes=[
            pltpu.VMEM((tm, d), BF16),
            pltpu.VMEM((tm, d), F32),
            pltpu.VMEM((nff * n_sub, SUBLANES, COL_SUB_FFN), F32),
            pltpu.VMEM((nff * n_sub, SUBLANES, COL_SUB_FFN), F32),
            pltpu.VMEM((n_sub, tm, COL_SUB_FFN), F32),
            pltpu.VMEM((n_sub, tm, COL_SUB_FFN), F32),
        ],
        compiler_params=_params(2),
        name="convglu_ffn",
    )(x2, g_pre, w_up, w_up, conv_w, conv_w, conv_b, conv_b, w_down, g_post)


def kernel(x, g_pre_mix, w_in, lam_re, lam_im, log_dt, b_re, b_im, c_re, c_im, d_skip, w_glu,
           w_up_ssm, w_up_attn, w_out, g_post_mix, g_pre_ffn, w_ffn_up, conv_w, conv_b,
           w_ffn_down, g_post_ffn):
    bsz, seq_len, d = x.shape
    depth = w_in.shape[0]
    d_ssm = w_glu.shape[1]
    d_attn = w_up_attn.shape[1]
    n_heads = d_attn // HEAD_DIM
    n_groups = d_ssm // SSM_GROUP
    assert seq_len % (ATTN_Q_BLOCKS * MOBA_BLOCK) == 0 and seq_len % SSM_CHUNK == 0
    assert d_ssm % d_attn == 0
    nb = seq_len // MOBA_BLOCK
    nc = seq_len // SSM_CHUNK
    n_rows = bsz * seq_len
    row = lambda v: v.reshape(1, -1)

    w_in_bf = w_in.astype(BF16)
    w_glu_bf = w_glu.astype(BF16)
    w_us_bf = w_up_ssm.astype(BF16)
    w_ua_bf = w_up_attn.astype(BF16)
    w_out_bf = w_out.astype(BF16)
    w_fu_bf = w_ffn_up.astype(BF16)
    w_fd_bf = w_ffn_down.astype(BF16)
    conv_b3 = conv_b[:, None, :]
    tables = jax.vmap(_ssm_tables)(lam_re, lam_im, log_dt, b_re, b_im, c_re, c_im, d_skip)
    col_scale = jnp.ones((w_in.shape[2],), F32).at[d_ssm:d_ssm + d_attn].set(1.0 / math.sqrt(HEAD_DIM))

    x2 = x.reshape(n_rows, d)
    for l in range(depth):
        proj2 = _inproj(x2, row(g_pre_mix[l]), w_in_bf, l, row(col_scale))

        hbar = _blockmean(x2, row(g_pre_mix[l])).reshape(bsz, 1, nb, d)
        hbar_t = jnp.broadcast_to(hbar, (bsz, n_heads, nb, d)).reshape(bsz * n_heads * nb, d)
        kbd = _kmean(hbar_t, w_in, l, nb, d_ssm, d_attn).reshape(bsz, n_heads * nb, d_attn)
        z = _gatevec(w_in, l, kbd, d_ssm)
        bias = _select(x2.reshape(bsz, seq_len, d), row(g_pre_mix[l]), z, nb)
        att = _attention(proj2.reshape(bsz, seq_len, -1), bias, nb, d_ssm)

        u_g = proj2[:, :d_ssm].reshape(bsz, nc, SSM_CHUNK, n_groups, SSM_GROUP)
        u_g = u_g.transpose(3, 0, 1, 2, 4).reshape(n_groups, bsz * nc, SSM_CHUNK * SSM_GROUP)
        y_g = _ssm(u_g, tables, l, bsz)
        y2 = y_g.reshape(n_groups, bsz, nc, SSM_CHUNK, SSM_GROUP).transpose(1, 2, 3, 0, 4)
        y2 = y2.reshape(n_rows, d_ssm)

        x2 = _merge(y2, att.reshape(n_rows, d_attn), proj2, x2, w_glu_bf, w_us_bf, w_ua_bf,
                    w_out_bf, l, row(g_post_mix[l]))
        x2 = _ffn(x2, row(g_pre_ffn[l]), w_fu_bf, conv_w, conv_b3, w_fd_bf, l,
                  row(g_post_ffn[l]), seq_len)
    return x2.reshape(bsz, seq_len, d)
```

```python
import functools
import math

import jax
import jax.numpy as jnp
from jax import lax
from jax.experimental import pallas as pl
from jax.experimental.pallas import tpu as pltpu

F32 = jnp.float32
BF16 = jnp.bfloat16

EPS = 1e-6
NEG = -1e30
SSM_GROUP = 16
HEAD_DIM = 128
MOBA_BLOCK = 256
MOBA_TOPK = 3
CONV_W = 3

LANES = 128
SUBLANES = 8
VMEM_LIMIT_BYTES = 56 * 1024 * 1024

SSM_CHUNK = 32
SSM_SUB = LANES // SSM_GROUP
ROW_TILE_PROJ = 1024
COL_TILE_PROJ = 1024
ROW_TILE_GATE = 512
ROW_TILE_MERGE = 512
ROW_TILE_FFN = 512
COL_TILE_FFN = 512
FFN_STRIP = 256
K_TILE_GATE = 512
NORM_ROWS = 128
ATTN_Q_BLOCKS = 2
ATTN_HEADS = 2

_NT = (((1,), (1,)), ((), ()))
_HI = lax.Precision.HIGHEST


def _params(n_axes):
    return pltpu.CompilerParams(
        dimension_semantics=("arbitrary",) * n_axes,
        vmem_limit_bytes=VMEM_LIMIT_BYTES,
    )


def _rmsnorm_rows(x, g):
    ms = jnp.mean(x * x, axis=-1, keepdims=True)
    return x * lax.rsqrt(ms + EPS) * g


def _rmsnorm_into(h_ref, x_ref, g_ref):
    step_rows = min(NORM_ROWS, x_ref.shape[0])

    def step(r, carry):
        r0 = pl.multiple_of(r * step_rows, step_rows)
        h_ref[pl.ds(r0, step_rows), :] = _rmsnorm_rows(
            x_ref[pl.ds(r0, step_rows), :], g_ref[...]).astype(h_ref.dtype)
        return carry

    lax.fori_loop(0, x_ref.shape[0] // step_rows, step, 0)


def _inproj_kernel(x_ref, g_ref, w_ref, cs_ref, o_ref, h_ref):
    @pl.when(pl.program_id(1) == 0)
    def _():
        _rmsnorm_into(h_ref, x_ref, g_ref)

    acc = jnp.dot(h_ref[...], w_ref[...], preferred_element_type=F32)
    o_ref[...] = (acc * cs_ref[...]).astype(o_ref.dtype)


def _inproj(x2, g, w_bf, layer, col_scale):
    n_rows, d = x2.shape
    d_in = w_bf.shape[2]
    tm = min(ROW_TILE_PROJ, n_rows)
    tn = COL_TILE_PROJ
    return pl.pallas_call(
        _inproj_kernel,
        grid=(n_rows // tm, d_in // tn),
        in_specs=[
            pl.BlockSpec((tm, d), lambda i, j: (i, 0)),
            pl.BlockSpec((1, d), lambda i, j: (0, 0)),
            pl.BlockSpec((None, d, tn), lambda i, j: (layer, 0, j)),
            pl.BlockSpec((1, tn), lambda i, j: (0, j)),
        ],
        out_specs=pl.BlockSpec((tm, tn), lambda i, j: (i, j)),
        out_shape=jax.ShapeDtypeStruct((n_rows, d_in), BF16),
        scratch_shapes=[pltpu.VMEM((tm, d), BF16)],
        compiler_params=_params(2),
        name="inproj",
    )(x2, g, w_bf, col_scale)


def _blockmean_kernel(x_ref, g_ref, o_ref):
    h = _rmsnorm_rows(x_ref[...], g_ref[...])
    o_ref[0] = jnp.mean(h, axis=0, keepdims=True)


def _blockmean(x2, g):
    n_rows, d = x2.shape
    n_blocks = n_rows // MOBA_BLOCK
    return pl.pallas_call(
        _blockmean_kernel,
        grid=(n_blocks,),
        in_specs=[
            pl.BlockSpec((MOBA_BLOCK, d), lambda i: (i, 0)),
            pl.BlockSpec((1, d), lambda i: (0, 0)),
        ],
        out_specs=pl.BlockSpec((1, 1, d), lambda i: (i, 0, 0)),
        out_shape=jax.ShapeDtypeStruct((n_blocks, 1, d), F32),
        compiler_params=_params(1),
        name="blockmean",
    )(x2, g)


def _kmean_kernel(hb_ref, wk_ref, o_ref, *, nb):
    @pl.when(pl.program_id(0) == 0)
    def _():
        o_ref[...] = jnp.zeros_like(o_ref)

    o_ref[...] += jnp.dot(hb_ref[...], wk_ref[...], preferred_element_type=F32, precision=_HI)

    @pl.when(pl.program_id(0) == pl.num_programs(0) - 1)
    def _():
        r = lax.broadcasted_iota(jnp.int32, o_ref.shape, 0)
        c = lax.broadcasted_iota(jnp.int32, o_ref.shape, 1)
        n_gate = (o_ref.shape[1] // HEAD_DIM) * nb
        keep = (c // HEAD_DIM) == ((r % n_gate) // nb)
        o_ref[...] = jnp.where(keep, o_ref[...], 0.0)


def _kmean(hbar_t, w_in, layer, nb, d_ssm, d_attn):
    rows, d = hbar_t.shape
    k_col_block = (d_ssm + d_attn) // d_attn
    tk = K_TILE_GATE
    return pl.pallas_call(
        functools.partial(_kmean_kernel, nb=nb),
        grid=(d // tk,),
        in_specs=[
            pl.BlockSpec((rows, tk), lambda kk: (0, kk)),
            pl.BlockSpec((None, tk, d_attn), lambda kk: (layer, kk, k_col_block)),
        ],
        out_specs=pl.BlockSpec((rows, d_attn), lambda kk: (0, 0)),
        out_shape=jax.ShapeDtypeStruct((rows, d_attn), F32),
        compiler_params=_params(1),
        name="kmean",
    )(hbar_t, w_in)


def _gatevec_kernel(wq_ref, kbd_ref, o_ref):
    o_ref[0] = lax.dot_general(wq_ref[...], kbd_ref[0], _NT, preferred_element_type=F32,
                               precision=_HI)


def _gatevec(w_in, layer, kbd, d_ssm):
    bsz, n_gate, d_attn = kbd.shape
    d = w_in.shape[1]
    q_col_block = d_ssm // d_attn
    tk = K_TILE_GATE
    return pl.pallas_call(
        _gatevec_kernel,
        grid=(bsz, d // tk),
        in_specs=[
            pl.BlockSpec((None, tk, d_attn), lambda b, kk: (layer, kk, q_col_block)),
            pl.BlockSpec((1, n_gate, d_attn), lambda b, kk: (b, 0, 0)),
        ],
        out_specs=pl.BlockSpec((1, tk, n_gate), lambda b, kk: (b, kk, 0)),
        out_shape=jax.ShapeDtypeStruct((bsz, d, n_gate), F32),
        compiler_params=_params(2),
        name="gatevec",
    )(w_in, kbd)


def _select_kernel(x_ref, g_ref, z_ref, o_ref, *, nb, tq):
    h = _rmsnorm_rows(x_ref[0], g_ref[...])
    gate = jnp.dot(h, z_ref[0], preferred_element_type=F32, precision=_HI)
    n_gate = gate.shape[1]
    row = lax.broadcasted_iota(jnp.int32, gate.shape, 0) + pl.program_id(1) * tq
    qblk = row // MOBA_BLOCK
    j = lax.broadcasted_iota(jnp.int32, gate.shape, 1) % nb
    past = j < qblk
    gate = jnp.where(past, gate, NEG)
    cnt = jnp.zeros(gate.shape, F32)
    for r in range(1, nb):
        lower = pltpu.roll(gate, r, 1)
        upper = pltpu.roll(gate, (r - nb) % n_gate, 1)
        has_lower = j >= r
        other = jnp.where(has_lower, lower, upper)
        wins_tie = jnp.where(other >= gate, 1.0, 0.0)
        wins_strict = jnp.where(other > gate, 1.0, 0.0)
        cnt = cnt + jnp.where(has_lower, wins_tie, wins_strict)
    keep = jnp.logical_or(jnp.logical_and(past, cnt < float(MOBA_TOPK)), j == qblk)
    o_ref[0] = jnp.where(keep, 0.0, NEG)


def _select(x3, g, z, nb):
    bsz, s, d = x3.shape
    n_gate = z.shape[2]
    tq = min(ROW_TILE_GATE, s)
    return pl.pallas_call(
        functools.partial(_select_kernel, nb=nb, tq=tq),
        grid=(bsz, s // tq),
        in_specs=[
            pl.BlockSpec((1, tq, d), lambda b, i: (b, i, 0)),
            pl.BlockSpec((1, d), lambda b, i: (0, 0)),
            pl.BlockSpec((1, d, n_gate), lambda b, i: (b, 0, 0)),
        ],
        out_specs=pl.BlockSpec((1, tq, n_gate), lambda b, i: (b, i, 0)),
        out_shape=jax.ShapeDtypeStruct((bsz, s, n_gate), F32),
        compiler_params=_params(2),
        name="select",
    )(x3, g, z)


def _attn_kernel(q_ref, k_ref, v_ref, bias_ref, o_ref, qa_ref, s_ref, m_ref, acc_ref, *, nb):
    hp = pl.program_id(1)
    i = pl.program_id(2)
    blk = MOBA_BLOCK
    tq = q_ref.shape[1]
    n_gate = bias_ref.shape[2]
    heads = range(ATTN_HEADS)
    mask_rows = bias_ref[0].astype(BF16)
    for hh in heads:
        qa_ref[hh, :, :HEAD_DIM] = q_ref[0, :, hh * HEAD_DIM:(hh + 1) * HEAD_DIM]
        qa_ref[hh, :, HEAD_DIM:] = mask_rows
    m_ref[...] = jnp.full(m_ref.shape, -jnp.inf, F32)
    acc_ref[...] = jnp.zeros(acc_ref.shape, F32)
    lane = lax.broadcasted_iota(jnp.int32, (blk, n_gate), 1)
    ones = jnp.ones((blk, HEAD_DIM), BF16)

    def scores(slot, hh, jb):
        st = pl.multiple_of(jb * blk, blk)
        col = (hp * ATTN_HEADS + hh) * nb + jb
        onehot = jnp.where(lane == col, 1.0, 0.0).astype(BF16)
        kj = k_ref[0, pl.ds(st, blk), hh * HEAD_DIM:(hh + 1) * HEAD_DIM]
        s_ref[slot, hh] = lax.dot_general(qa_ref[hh], jnp.concatenate([kj, onehot], axis=1), _NT,
                                          preferred_element_type=F32)

    def update(slot, hh, jb, causal):
        s = s_ref[slot, hh]
        if causal:
            qpos = lax.broadcasted_iota(jnp.int32, s.shape, 0) + i * tq
            kpos = lax.broadcasted_iota(jnp.int32, s.shape, 1) + jb * blk
            s = jnp.where(kpos <= qpos, s, NEG)
        st = pl.multiple_of(jb * blk, blk)
        m_old = m_ref[hh]
        m_new = jnp.maximum(m_old, jnp.max(s, axis=-1, keepdims=True))
        alpha = jnp.exp(m_old - m_new)
        p = jnp.exp(s - jnp.concatenate([m_new] * (blk // LANES), axis=1))
        vj = v_ref[0, pl.ds(st, blk), hh * HEAD_DIM:(hh + 1) * HEAD_DIM]
        acc_ref[hh] = jnp.concatenate([alpha, alpha], axis=1) * acc_ref[hh] + jnp.dot(
            p.astype(BF16), jnp.concatenate([vj, ones], axis=1), preferred_element_type=F32)
        m_ref[hh] = m_new

    for hh in heads:
        scores(0, hh, 0)

    def pair(jj, carry):
        for hh in heads:
            scores(1, hh, 2 * jj + 1)
        for hh in heads:
            update(0, hh, 2 * jj, False)
        for hh in heads:
            scores(0, hh, 2 * jj + 2)
        for hh in heads:
            update(1, hh, 2 * jj + 1, False)
        return carry

    lax.fori_loop(0, i, pair, 0)
    for hh in heads:
        scores(1, hh, 2 * i + 1)
    for hh in heads:
        update(0, hh, 2 * i, True)
    for hh in heads:
        update(1, hh, 2 * i + 1, True)
    for hh in heads:
        o_ref[0, :, hh * HEAD_DIM:(hh + 1) * HEAD_DIM] = (
            acc_ref[hh, :, :HEAD_DIM] / acc_ref[hh, :, HEAD_DIM:]).astype(o_ref.dtype)


def _attention(proj3, bias, nb, d_ssm):
    bsz, s, _ = proj3.shape
    n_gate = bias.shape[2]
    n_heads = n_gate // nb
    d_attn = n_heads * HEAD_DIM
    q0 = d_ssm // HEAD_DIM
    k0 = q0 + n_heads
    v0 = k0 + n_heads
    assert ATTN_Q_BLOCKS == 2 and nb % ATTN_Q_BLOCKS == 0
    assert n_heads % ATTN_HEADS == 0 and q0 % ATTN_HEADS == 0
    tq = ATTN_Q_BLOCKS * MOBA_BLOCK
    hw = ATTN_HEADS * HEAD_DIM
    return pl.pallas_call(
        functools.partial(_attn_kernel, nb=nb),
        grid=(bsz, n_heads // ATTN_HEADS, s // tq),
        in_specs=[
            pl.BlockSpec((1, tq, hw), lambda b, h, i: (b, i, q0 // ATTN_HEADS + h)),
            pl.BlockSpec((1, s, hw), lambda b, h, i: (b, 0, k0 // ATTN_HEADS + h)),
            pl.BlockSpec((1, s, hw), lambda b, h, i: (b, 0, v0 // ATTN_HEADS + h)),
            pl.BlockSpec((1, tq, n_gate), lambda b, h, i: (b, i, 0)),
        ],
        out_specs=pl.BlockSpec((1, tq, hw), lambda b, h, i: (b, i, h)),
        out_shape=jax.ShapeDtypeStruct((bsz, s, d_attn), BF16),
        scratch_shapes=[
            pltpu.VMEM((ATTN_HEADS, tq, HEAD_DIM + n_gate), BF16),
            pltpu.VMEM((2, ATTN_HEADS, tq, MOBA_BLOCK), F32),
            pltpu.VMEM((ATTN_HEADS, tq, LANES), F32),
            pltpu.VMEM((ATTN_HEADS, tq, 2 * HEAD_DIM), F32),
        ],
        compiler_params=_params(3),
        name="moba_attn",
    )(proj3, proj3, proj3, bias)


def _ssm_tables(lam_re, lam_im, log_dt, b_re, b_im, c_re, c_im, d_skip):
    t_len = SSM_CHUNK
    n_groups, n_state = lam_re.shape
    n_ch = SSM_GROUP
    assert 2 * n_state == LANES
    w = t_len * n_ch
    dt = jnp.exp(log_dt)[:, None]
    ar = lam_re * dt
    ai = lam_im * dt
    steps = jnp.arange(t_len + 1, dtype=F32)[None, :, None]
    mag = jnp.exp(ar[:, None, :] * steps)
    pw_re = mag * jnp.cos(ai[:, None, :] * steps)
    pw_im = mag * jnp.sin(ai[:, None, :] * steps)
    e1 = jnp.expm1(ar)
    sh = jnp.sin(0.5 * ai)
    num_re = e1 * jnp.cos(ai) - 2.0 * sh * sh
    num_im = (e1 + 1.0) * jnp.sin(ai)
    den = lam_re * lam_re + lam_im * lam_im
    coef_re = (num_re * lam_re + num_im * lam_im) / den
    coef_im = (num_im * lam_re - num_re * lam_im) / den
    bb_re = coef_re[..., None] * b_re - coef_im[..., None] * b_im
    bb_im = coef_re[..., None] * b_im + coef_im[..., None] * b_re
    cp_re = c_re[:, None] * pw_re[:, :, None, :] - c_im[:, None] * pw_im[:, :, None, :]
    cp_im = c_re[:, None] * pw_im[:, :, None, :] + c_im[:, None] * pw_re[:, :, None, :]
    cpow = jnp.concatenate([cp_re, -cp_im], axis=-1).reshape(n_groups, (t_len + 1) * n_ch, LANES)
    bbar = jnp.pad(jnp.concatenate([bb_re, bb_im], axis=1), ((0, 0), (0, 0), (0, LANES - n_ch)))
    rev_re = pw_re[:, :t_len][:, ::-1][:, :, None, :]
    rev_im = pw_im[:, :t_len][:, ::-1][:, :, None, :]
    bt_re = bb_re.transpose(0, 2, 1)[:, None]
    bt_im = bb_im.transpose(0, 2, 1)[:, None]
    pin = jnp.stack([rev_re * bt_re - rev_im * bt_im, rev_re * bt_im + rev_im * bt_re], axis=1)
    pad_p = LANES - n_state
    pin = jnp.pad(pin.reshape(n_groups, 2, w, n_state), ((0, 0), (0, 0), (0, 0), (0, pad_p)))
    adec = jnp.stack([pw_re[:, t_len], pw_im[:, t_len]], axis=1)[:, :, None, :]
    adec = jnp.pad(adec, ((0, 0), (0, 0), (0, 0), (0, pad_p)))
    dtile = jnp.tile(d_skip.reshape(n_groups, 1, n_ch), (1, 1, SSM_SUB))
    return cpow, bbar, pin.astype(BF16), adec, dtile


def _ssm_kernel(u_ref, cpow_ref, bbar_ref, pin_ref, adec_ref, dtile_ref, o_ref,
                hre_ref, him_ref, kk_ref, tt_ref, *, bsz, nc):
    w = u_ref.shape[2]
    n_state = LANES // 2
    n_sub = w // LANES
    u = u_ref[0]

    kk_ref[0:LANES, :] = jnp.zeros((LANES, LANES), F32)
    kk_ref[LANES:, :] = jnp.dot(cpow_ref[0, 0:w, :], bbar_ref[0], preferred_element_type=F32,
                                precision=_HI)
    row = lax.broadcasted_iota(jnp.int32, (LANES, LANES), 0)
    col = lax.broadcasted_iota(jnp.int32, (LANES, LANES), 1)
    tt_ref[...] = jnp.zeros(tt_ref.shape, BF16)
    for d in range(n_sub):
        tile = None
        for sp in range(SSM_SUB):
            start = LANES + (SSM_SUB * d - sp) * SSM_GROUP
            piece = kk_ref[start:start + LANES, :]
            if sp:
                piece = pltpu.roll(piece, sp * SSM_GROUP, 1)
            tile = piece if tile is None else tile + piece
        if d == 0:
            tile = tile + jnp.where(row == col, dtile_ref[0], 0.0)
        tile = tile.astype(BF16)
        for sb in range(n_sub - d):
            tb = sb + d
            tt_ref[tb * LANES:(tb + 1) * LANES, sb * LANES:(sb + 1) * LANES] = tile

    hre_ref[...] = jnp.dot(u, pin_ref[0, 0], preferred_element_type=F32)
    him_ref[...] = jnp.dot(u, pin_ref[0, 1], preferred_element_type=F32)
    a_re = adec_ref[0, 0]
    a_im = adec_ref[0, 1]

    def step(c, carry):
        new = []
        for b in range(bsz):
            s_re, s_im = carry[b]
            r = b * nc + c
            loc_re = hre_ref[pl.ds(r, 1), :]
            loc_im = him_ref[pl.ds(r, 1), :]
            hre_ref[pl.ds(r, 1), :] = s_re
            him_ref[pl.ds(r, 1), :] = s_im
            new.append((a_re * s_re - a_im * s_im + loc_re,
                        a_re * s_im + a_im * s_re + loc_im))
        return tuple(new)

    zero = jnp.zeros((1, LANES), F32)
    lax.fori_loop(0, nc, step, tuple((zero, zero) for _ in range(bsz)), unroll=4)

    y = lax.dot_general(u, tt_ref[...], _NT, preferred_element_type=F32)
    h_in = (hre_ref[...] + pltpu.roll(him_ref[...], n_state, 1)).astype(BF16)
    c_next = cpow_ref[0, SSM_GROUP:SSM_GROUP + w, :].astype(BF16)
    y = y + lax.dot_general(h_in, c_next, _NT, preferred_element_type=F32)
    o_ref[0] = jax.nn.gelu(y).astype(o_ref.dtype)


def _ssm(u_g, tables, layer, bsz):
    cpow, bbar, pin, adec, dtile = tables
    n_groups, rows, w = u_g.shape
    nc = rows // bsz
    cp_rows = cpow.shape[2]
    return pl.pallas_call(
        functools.partial(_ssm_kernel, bsz=bsz, nc=nc),
        grid=(n_groups,),
        in_specs=[
            pl.BlockSpec((1, rows, w), lambda g: (g, 0, 0)),
            pl.BlockSpec((None, 1, cp_rows, LANES), lambda g: (layer, g, 0, 0)),
            pl.BlockSpec((None, 1, LANES, LANES), lambda g: (layer, g, 0, 0)),
            pl.BlockSpec((None, 1, 2, w, LANES), lambda g: (layer, g, 0, 0, 0)),
            pl.BlockSpec((None, 1, 2, 1, LANES), lambda g: (layer, g, 0, 0, 0)),
            pl.BlockSpec((None, 1, 1, LANES), lambda g: (layer, g, 0, 0)),
        ],
        out_specs=pl.BlockSpec((1, rows, w), lambda g: (g, 0, 0)),
        out_shape=jax.ShapeDtypeStruct((n_groups, rows, w), BF16),
        scratch_shapes=[
            pltpu.VMEM((rows, LANES), F32),
            pltpu.VMEM((rows, LANES), F32),
            pltpu.VMEM((LANES + w, LANES), F32),
            pltpu.VMEM((w, w), BF16),
        ],
        compiler_params=_params(1),
        name="s5_chunked",
    )(u_g, cpow, bbar, pin, adec, dtile)


def _merge_kernel(y_ref, att_ref, ga_ref, gb_ref, x_ref, wglu_ref, wus_ref, wua_ref, wout_ref,
                  g_ref, o_ref):
    y = y_ref[...]
    z = jnp.dot(y, wglu_ref[...], preferred_element_type=F32)
    s5 = (y.astype(F32) * jax.nn.sigmoid(z)).astype(BF16)
    ys = jnp.dot(s5, wus_ref[...], preferred_element_type=F32)
    ya = jnp.dot(att_ref[...], wua_ref[...], preferred_element_type=F32)
    m = (jax.nn.sigmoid(ga_ref[...].astype(F32)) * ys
         + jax.nn.sigmoid(gb_ref[...].astype(F32)) * ya)
    o = jnp.dot(m.astype(BF16), wout_ref[...], preferred_element_type=F32)
    o_ref[...] = x_ref[...] + _rmsnorm_rows(o, g_ref[...])


def _merge(y2, att2, proj2, x2, w_glu, w_us, w_ua, w_out, layer, g_post):
    n_rows, d = x2.shape
    d_ssm = y2.shape[1]
    d_attn = att2.shape[1]
    tm = min(ROW_TILE_MERGE, n_rows)
    ga_blk = (d_ssm + 3 * d_attn) // d

    def weight(wt):
        return pl.BlockSpec((None,) + wt.shape[1:], lambda i: (layer, 0, 0),
                            pipeline_mode=pl.Buffered(1))

    return pl.pallas_call(
        _merge_kernel,
        grid=(n_rows // tm,),
        in_specs=[
            pl.BlockSpec((tm, d_ssm), lambda i: (i, 0)),
            pl.BlockSpec((tm, d_attn), lambda i: (i, 0)),
            pl.BlockSpec((tm, d), lambda i: (i, ga_blk)),
            pl.BlockSpec((tm, d), lambda i: (i, ga_blk + 1)),
            pl.BlockSpec((tm, d), lambda i: (i, 0)),
            weight(w_glu), weight(w_us), weight(w_ua), weight(w_out),
            pl.BlockSpec((1, d), lambda i: (0, 0)),
        ],
        out_specs=pl.BlockSpec((tm, d), lambda i: (i, 0)),
        out_shape=jax.ShapeDtypeStruct((n_rows, d), F32),
        compiler_params=_params(1),
        name="merge",
    )(y2, att2, proj2, proj2, x2, w_glu, w_us, w_ua, w_out, g_post)


def _ffn_kernel(xa_ref, xb_ref, gpre_ref, wa_ref, wv_ref, cwa_ref, cwv_ref, cba_ref, cbv_ref,
                wd_ref, gpost_ref, o_ref, h_ref, acc_ref, za0_ref, za1_ref, zv0_ref, zv1_ref,
                halo_a_ref, halo_v_ref, *, nff, n_steps, tiles_per_seq):
    t = pl.program_id(0)
    tm = xa_ref.shape[0]
    ta = jnp.minimum(t, n_steps - 1)
    tb = jnp.maximum(t - 1, 0)
    ca = ta % nff
    ib = tb // nff
    cb = tb % nff

    @pl.when(jnp.logical_and(ca == 0, t < n_steps))
    def _():
        _rmsnorm_into(h_ref, xa_ref, gpre_ref)

    @pl.when(t == 0)
    def _():
        za1_ref[...] = jnp.zeros_like(za1_ref)
        zv1_ref[...] = jnp.zeros_like(zv1_ref)
        acc_ref[...] = jnp.zeros_like(acc_ref)

    @pl.when((ib % tiles_per_seq) == 0)
    def _():
        halo_a_ref[cb] = jnp.zeros(halo_a_ref.shape[1:], F32)
        halo_v_ref[cb] = jnp.zeros(halo_v_ref.shape[1:], F32)

    def conv_strip(z_ref, cw, bias, r0):
        ext = z_ref[r0:r0 + FFN_STRIP + SUBLANES, :]
        z1 = pltpu.roll(ext, 1, 0)[SUBLANES:, :]
        z2 = pltpu.roll(ext, 2, 0)[SUBLANES:, :]
        return cw[0:1, :] * z2 + cw[1:2, :] * z1 + cw[2:3, :] * ext[SUBLANES:, :] + bias

    def step(za_new, zv_new, za_old, zv_old):
        za_old[0:SUBLANES, :] = halo_a_ref[cb]
        zv_old[0:SUBLANES, :] = halo_v_ref[cb]
        halo_a_ref[cb] = za_old[tm:tm + SUBLANES, :]
        halo_v_ref[cb] = zv_old[tm:tm + SUBLANES, :]
        cwa, cwv = cwa_ref[...], cwv_ref[...]
        cba, cbv = cba_ref[...], cbv_ref[...]
        for r0 in range(0, tm, FFN_STRIP):
            rows = slice(r0, r0 + FFN_STRIP)
            out_rows = slice(SUBLANES + r0, SUBLANES + r0 + FFN_STRIP)
            h = h_ref[rows, :]
            za_new[out_rows, :] = jnp.dot(h, wa_ref[...], preferred_element_type=F32)
            zv_new[out_rows, :] = jnp.dot(h, wv_ref[...], preferred_element_type=F32)
            a = conv_strip(za_old, cwa, cba, r0)
            v = conv_strip(zv_old, cwv, cbv, r0)
            gated = (a * jax.nn.sigmoid(a) * v).astype(BF16)
            acc_ref[rows, :] += jnp.dot(gated, wd_ref[...], preferred_element_type=F32)

    @pl.when(t % 2 == 0)
    def _():
        step(za0_ref, zv0_ref, za1_ref, zv1_ref)

    @pl.when(t % 2 == 1)
    def _():
        step(za1_ref, zv1_ref, za0_ref, zv0_ref)

    @pl.when(jnp.logical_and(t > 0, cb == nff - 1))
    def _():
        o_ref[...] = xb_ref[...] + _rmsnorm_rows(acc_ref[...], gpost_ref[...])

    @pl.when(jnp.logical_or(t == 0, cb == nff - 1))
    def _():
        acc_ref[...] = jnp.zeros_like(acc_ref)


def _ffn(x2, g_pre, w_up, conv_w, conv_b, w_down, layer, g_post, seq_len):
    n_rows, d = x2.shape
    d_ff = w_down.shape[1]
    tm = min(ROW_TILE_FFN, seq_len)
    tf = COL_TILE_FFN
    nff = d_ff // tf
    assert tm % FFN_STRIP == 0
    n_steps = (n_rows // tm) * nff
    const = lambda t: (0, 0)
    pa = lambda t: jnp.minimum(t, n_steps - 1)
    pb = lambda t: jnp.maximum(t - 1, 0)
    zbuf = pltpu.VMEM((SUBLANES + tm, tf), F32)
    return pl.pallas_call(
        functools.partial(_ffn_kernel, nff=nff, n_steps=n_steps, tiles_per_seq=seq_len // tm),
        grid=(n_steps + 1,),
        in_specs=[
            pl.BlockSpec((tm, d), lambda t: (pa(t) // nff, 0)),
            pl.BlockSpec((tm, d), lambda t: (pb(t) // nff, 0), pipeline_mode=pl.Buffered(1)),
            pl.BlockSpec((1, d), const),
            pl.BlockSpec((None, d, tf), lambda t: (layer, 0, pa(t) % nff)),
            pl.BlockSpec((None, d, tf), lambda t: (layer, 0, nff + pa(t) % nff)),
            pl.BlockSpec((None, CONV_W, tf), lambda t: (layer, 0, pb(t) % nff)),
            pl.BlockSpec((None, CONV_W, tf), lambda t: (layer, 0, nff + pb(t) % nff)),
            pl.BlockSpec((None, 1, tf), lambda t: (layer, 0, pb(t) % nff)),
            pl.BlockSpec((None, 1, tf), lambda t: (layer, 0, nff + pb(t) % nff)),
            pl.BlockSpec((None, tf, d), lambda t: (layer, pb(t) % nff, 0)),
            pl.BlockSpec((1, d), const),
        ],
        out_specs=pl.BlockSpec((tm, d), lambda t: (pb(t) // nff, 0)),
        out_shape=jax.ShapeDtypeStruct((n_rows, d), F32),
        scratch_shapes=[
            pltpu.VMEM((tm, d), BF16),
            pltpu.VMEM((tm, d), F32),
            zbuf, zbuf,
            zbuf, zbuf,
            pltpu.VMEM((nff, SUBLANES, tf), F32),
            pltpu.VMEM((nff, SUBLANES, tf), F32),
        ],
        compiler_params=_params(1),
        name="convglu_ffn",
    )(x2, x2, g_pre, w_up, w_up, conv_w, conv_w, conv_b, conv_b, w_down, g_post)


def kernel(x, g_pre_mix, w_in, lam_re, lam_im, log_dt, b_re, b_im, c_re, c_im, d_skip, w_glu,
           w_up_ssm, w_up_attn, w_out, g_post_mix, g_pre_ffn, w_ffn_up, conv_w, conv_b,
           w_ffn_down, g_post_ffn):
    bsz, seq_len, d = x.shape
    depth = w_in.shape[0]
    d_ssm = w_glu.shape[1]
    d_attn = w_up_attn.shape[1]
    n_heads = d_attn // HEAD_DIM
    n_groups = d_ssm // SSM_GROUP
    assert seq_len % (ATTN_Q_BLOCKS * MOBA_BLOCK) == 0 and seq_len % SSM_CHUNK == 0
    assert d_ssm % d_attn == 0
    nb = seq_len // MOBA_BLOCK
    nc = seq_len // SSM_CHUNK
    n_rows = bsz * seq_len
    row = lambda v: v.reshape(1, -1)

    w_in_bf = w_in.astype(BF16)
    w_glu_bf = w_glu.astype(BF16)
    w_us_bf = w_up_ssm.astype(BF16)
    w_ua_bf = w_up_attn.astype(BF16)
    w_out_bf = w_out.astype(BF16)
    w_fu_bf = w_ffn_up.astype(BF16)
    w_fd_bf = w_ffn_down.astype(BF16)
    conv_b3 = conv_b[:, None, :]
    tables = jax.vmap(_ssm_tables)(lam_re, lam_im, log_dt, b_re, b_im, c_re, c_im, d_skip)
    col_scale = jnp.ones((w_in.shape[2],), F32).at[d_ssm:d_ssm + d_attn].set(1.0 / math.sqrt(HEAD_DIM))

    x2 = x.reshape(n_rows, d)
    for l in range(depth):
        proj2 = _inproj(x2, row(g_pre_mix[l]), w_in_bf, l, row(col_scale))

        hbar = _blockmean(x2, row(g_pre_mix[l])).reshape(bsz, 1, nb, d)
        hbar_t = jnp.broadcast_to(hbar, (bsz, n_heads, nb, d)).reshape(bsz * n_heads * nb, d)
        kbd = _kmean(hbar_t, w_in, l, nb, d_ssm, d_attn).reshape(bsz, n_heads * nb, d_attn)
        z = _gatevec(w_in, l, kbd, d_ssm)
        bias = _select(x2.reshape(bsz, seq_len, d), row(g_pre_mix[l]), z, nb)
        att = _attention(proj2.reshape(bsz, seq_len, -1), bias, nb, d_ssm)

        u_g = proj2[:, :d_ssm].reshape(bsz, nc, SSM_CHUNK, n_groups, SSM_GROUP)
        u_g = u_g.transpose(3, 0, 1, 2, 4).reshape(n_groups, bsz * nc, SSM_CHUNK * SSM_GROUP)
        y_g = _ssm(u_g, tables, l, bsz)
        y2 = y_g.reshape(n_groups, bsz, nc, SSM_CHUNK, SSM_GROUP).transpose(1, 2, 3, 0, 4)
        y2 = y2.reshape(n_rows, d_ssm)

        x2 = _merge(y2, att.reshape(n_rows, d_attn), proj2, x2, w_glu_bf, w_us_bf, w_ua_bf,
                    w_out_bf, l, row(g_post_mix[l]))
        x2 = _ffn(x2, row(g_pre_ffn[l]), w_fu_bf, conv_w, conv_b3, w_fd_bf, l,
                  row(g_post_ffn[l]), seq_len)
    return x2.reshape(bsz, seq_len, d)
```

```python
import functools
import math

import jax
import jax.numpy as jnp
from jax import lax
from jax.experimental import pallas as pl
from jax.experimental.pallas import tpu as pltpu

F32 = jnp.float32
BF16 = jnp.bfloat16

EPS = 1e-6
NEG = -1e30
SSM_GROUP = 16
HEAD_DIM = 128
MOBA_BLOCK = 256
MOBA_TOPK = 3
CONV_W = 3

LANES = 128
SUBLANES = 8
VMEM_LIMIT_BYTES = 56 * 1024 * 1024

SSM_CHUNK = 32
SSM_LANE_GROUPS = LANES // SSM_GROUP
SSM_STATE_POS = 8
ROW_TILE_PROJ = 1024
COL_TILE_PROJ = 1024
ROW_TILE_GATE = 512
ROW_TILE_MERGE = 512
ROW_TILE_FFN = 512
COL_TILE_FFN = 512
FFN_STRIP = 256
K_TILE_GATE = 512
NORM_ROWS = 128
ATTN_Q_BLOCKS = 2
ATTN_HEADS = 2

_NT = (((1,), (1,)), ((), ()))
_HI = lax.Precision.HIGHEST


def _params(n_axes):
    return pltpu.CompilerParams(
        dimension_semantics=("arbitrary",) * n_axes,
        vmem_limit_bytes=VMEM_LIMIT_BYTES,
    )


def _rmsnorm_rows(x, g):
    ms = jnp.mean(x * x, axis=-1, keepdims=True)
    return x * lax.rsqrt(ms + EPS) * g


def _rmsnorm_into(h_ref, x_ref, g_ref):
    step_rows = min(NORM_ROWS, x_ref.shape[0])

    def step(r, carry):
        r0 = pl.multiple_of(r * step_rows, step_rows)
        h_ref[pl.ds(r0, step_rows), :] = _rmsnorm_rows(
            x_ref[pl.ds(r0, step_rows), :], g_ref[...]).astype(h_ref.dtype)
        return carry

    lax.fori_loop(0, x_ref.shape[0] // step_rows, step, 0)


def _inproj_kernel(x_ref, g_ref, w_ref, cs_ref, o_ref, h_ref):
    @pl.when(pl.program_id(1) == 0)
    def _():
        _rmsnorm_into(h_ref, x_ref, g_ref)

    acc = jnp.dot(h_ref[...], w_ref[...], preferred_element_type=F32)
    o_ref[...] = (acc * cs_ref[...]).astype(o_ref.dtype)


def _inproj(x2, g, w_bf, layer, col_scale):
    n_rows, d = x2.shape
    d_in = w_bf.shape[2]
    tm = min(ROW_TILE_PROJ, n_rows)
    tn = COL_TILE_PROJ
    return pl.pallas_call(
        _inproj_kernel,
        grid=(n_rows // tm, d_in // tn),
        in_specs=[
            pl.BlockSpec((tm, d), lambda i, j: (i, 0)),
            pl.BlockSpec((1, d), lambda i, j: (0, 0)),
            pl.BlockSpec((None, d, tn), lambda i, j: (layer, 0, j)),
            pl.BlockSpec((1, tn), lambda i, j: (0, j)),
        ],
        out_specs=pl.BlockSpec((tm, tn), lambda i, j: (i, j)),
        out_shape=jax.ShapeDtypeStruct((n_rows, d_in), BF16),
        scratch_shapes=[pltpu.VMEM((tm, d), BF16)],
        compiler_params=_params(2),
        name="inproj",
    )(x2, g, w_bf, col_scale)


def _blockmean_kernel(x_ref, g_ref, o_ref):
    h = _rmsnorm_rows(x_ref[...], g_ref[...])
    o_ref[0] = jnp.mean(h, axis=0, keepdims=True)


def _blockmean(x2, g):
    n_rows, d = x2.shape
    n_blocks = n_rows // MOBA_BLOCK
    return pl.pallas_call(
        _blockmean_kernel,
        grid=(n_blocks,),
        in_specs=[
            pl.BlockSpec((MOBA_BLOCK, d), lambda i: (i, 0)),
            pl.BlockSpec((1, d), lambda i: (0, 0)),
        ],
        out_specs=pl.BlockSpec((1, 1, d), lambda i: (i, 0, 0)),
        out_shape=jax.ShapeDtypeStruct((n_blocks, 1, d), F32),
        compiler_params=_params(1),
        name="blockmean",
    )(x2, g)


def _kmean_kernel(hb_ref, wk_ref, o_ref, *, nb):
    @pl.when(pl.program_id(0) == 0)
    def _():
        o_ref[...] = jnp.zeros_like(o_ref)

    o_ref[...] += jnp.dot(hb_ref[...], wk_ref[...], preferred_element_type=F32, precision=_HI)

    @pl.when(pl.program_id(0) == pl.num_programs(0) - 1)
    def _():
        r = lax.broadcasted_iota(jnp.int32, o_ref.shape, 0)
        c = lax.broadcasted_iota(jnp.int32, o_ref.shape, 1)
        n_gate = (o_ref.shape[1] // HEAD_DIM) * nb
        keep = (c // HEAD_DIM) == ((r % n_gate) // nb)
        o_ref[...] = jnp.where(keep, o_ref[...], 0.0)


def _kmean(hbar_t, w_in, layer, nb, d_ssm, d_attn):
    rows, d = hbar_t.shape
    k_col_block = (d_ssm + d_attn) // d_attn
    tk = K_TILE_GATE
    return pl.pallas_call(
        functools.partial(_kmean_kernel, nb=nb),
        grid=(d // tk,),
        in_specs=[
            pl.BlockSpec((rows, tk), lambda kk: (0, kk)),
            pl.BlockSpec((None, tk, d_attn), lambda kk: (layer, kk, k_col_block)),
        ],
        out_specs=pl.BlockSpec((rows, d_attn), lambda kk: (0, 0)),
        out_shape=jax.ShapeDtypeStruct((rows, d_attn), F32),
        compiler_params=_params(1),
        name="kmean",
    )(hbar_t, w_in)


def _gatevec_kernel(wq_ref, kbd_ref, o_ref):
    o_ref[0] = lax.dot_general(wq_ref[...], kbd_ref[0], _NT, preferred_element_type=F32,
                               precision=_HI)


def _gatevec(w_in, layer, kbd, d_ssm):
    bsz, n_gate, d_attn = kbd.shape
    d = w_in.shape[1]
    q_col_block = d_ssm // d_attn
    tk = K_TILE_GATE
    return pl.pallas_call(
        _gatevec_kernel,
        grid=(bsz, d // tk),
        in_specs=[
            pl.BlockSpec((None, tk, d_attn), lambda b, kk: (layer, kk, q_col_block)),
            pl.BlockSpec((1, n_gate, d_attn), lambda b, kk: (b, 0, 0)),
        ],
        out_specs=pl.BlockSpec((1, tk, n_gate), lambda b, kk: (b, kk, 0)),
        out_shape=jax.ShapeDtypeStruct((bsz, d, n_gate), F32),
        compiler_params=_params(2),
        name="gatevec",
    )(w_in, kbd)


def _select_kernel(x_ref, g_ref, z_ref, o_ref, *, nb, tq):
    h = _rmsnorm_rows(x_ref[0], g_ref[...])
    gate = jnp.dot(h, z_ref[0], preferred_element_type=F32, precision=_HI)
    n_gate = gate.shape[1]
    row = lax.broadcasted_iota(jnp.int32, gate.shape, 0) + pl.program_id(1) * tq
    qblk = row // MOBA_BLOCK
    j = lax.broadcasted_iota(jnp.int32, gate.shape, 1) % nb
    past = j < qblk
    gate = jnp.where(past, gate, NEG)
    cnt = jnp.zeros(gate.shape, F32)
    for r in range(1, nb):
        lower = pltpu.roll(gate, r, 1)
        upper = pltpu.roll(gate, (r - nb) % n_gate, 1)
        has_lower = j >= r
        other = jnp.where(has_lower, lower, upper)
        wins_tie = jnp.where(other >= gate, 1.0, 0.0)
        wins_strict = jnp.where(other > gate, 1.0, 0.0)
        cnt = cnt + jnp.where(has_lower, wins_tie, wins_strict)
    keep = jnp.logical_or(jnp.logical_and(past, cnt < float(MOBA_TOPK)), j == qblk)
    o_ref[0] = jnp.where(keep, 0.0, NEG)


def _select(x3, g, z, nb):
    bsz, s, d = x3.shape
    n_gate = z.shape[2]
    tq = min(ROW_TILE_GATE, s)
    return pl.pallas_call(
        functools.partial(_select_kernel, nb=nb, tq=tq),
        grid=(bsz, s // tq),
        in_specs=[
            pl.BlockSpec((1, tq, d), lambda b, i: (b, i, 0)),
            pl.BlockSpec((1, d), lambda b, i: (0, 0)),
            pl.BlockSpec((1, d, n_gate), lambda b, i: (b, 0, 0)),
        ],
        out_specs=pl.BlockSpec((1, tq, n_gate), lambda b, i: (b, i, 0)),
        out_shape=jax.ShapeDtypeStruct((bsz, s, n_gate), F32),
        compiler_params=_params(2),
        name="select",
    )(x3, g, z)


def _attn_kernel(q_ref, k_ref, v_ref, bias_ref, o_ref, qa_ref, s_ref, m_ref, acc_ref, *, nb):
    hp = pl.program_id(1)
    i = pl.program_id(2)
    blk = MOBA_BLOCK
    tq = q_ref.shape[1]
    n_gate = bias_ref.shape[2]
    heads = range(ATTN_HEADS)
    mask_rows = bias_ref[0].astype(BF16)
    for hh in heads:
        qa_ref[hh, :, :HEAD_DIM] = q_ref[0, :, hh * HEAD_DIM:(hh + 1) * HEAD_DIM]
        qa_ref[hh, :, HEAD_DIM:] = mask_rows
    m_ref[...] = jnp.full(m_ref.shape, -jnp.inf, F32)
    acc_ref[...] = jnp.zeros(acc_ref.shape, F32)
    lane = lax.broadcasted_iota(jnp.int32, (blk, n_gate), 1)
    ones = jnp.ones((blk, HEAD_DIM), BF16)

    def scores(slot, hh, jb):
        st = pl.multiple_of(jb * blk, blk)
        col = (hp * ATTN_HEADS + hh) * nb + jb
        onehot = jnp.where(lane == col, 1.0, 0.0).astype(BF16)
        kj = k_ref[0, pl.ds(st, blk), hh * HEAD_DIM:(hh + 1) * HEAD_DIM]
        s_ref[slot, hh] = lax.dot_general(qa_ref[hh], jnp.concatenate([kj, onehot], axis=1), _NT,
                                          preferred_element_type=F32)

    def update(slot, hh, jb, causal):
        s = s_ref[slot, hh]
        if causal:
            qpos = lax.broadcasted_iota(jnp.int32, s.shape, 0) + i * tq
            kpos = lax.broadcasted_iota(jnp.int32, s.shape, 1) + jb * blk
            s = jnp.where(kpos <= qpos, s, NEG)
        st = pl.multiple_of(jb * blk, blk)
        m_old = m_ref[hh]
        m_new = jnp.maximum(m_old, jnp.max(s, axis=-1, keepdims=True))
        alpha = jnp.exp(m_old - m_new)
        p = jnp.exp(s - jnp.concatenate([m_new] * (blk // LANES), axis=1))
        vj = v_ref[0, pl.ds(st, blk), hh * HEAD_DIM:(hh + 1) * HEAD_DIM]
        acc_ref[hh] = jnp.concatenate([alpha, alpha], axis=1) * acc_ref[hh] + jnp.dot(
            p.astype(BF16), jnp.concatenate([vj, ones], axis=1), preferred_element_type=F32)
        m_ref[hh] = m_new

    for hh in heads:
        scores(0, hh, 0)

    def pair(jj, carry):
        for hh in heads:
            scores(1, hh, 2 * jj + 1)
        for hh in heads:
            update(0, hh, 2 * jj, False)
        for hh in heads:
            scores(0, hh, 2 * jj + 2)
        for hh in heads:
            update(1, hh, 2 * jj + 1, False)
        return carry

    lax.fori_loop(0, i, pair, 0)
    for hh in heads:
        scores(1, hh, 2 * i + 1)
    for hh in heads:
        update(0, hh, 2 * i, True)
    for hh in heads:
        update(1, hh, 2 * i + 1, True)
    for hh in heads:
        o_ref[0, :, hh * HEAD_DIM:(hh + 1) * HEAD_DIM] = (
            acc_ref[hh, :, :HEAD_DIM] / acc_ref[hh, :, HEAD_DIM:]).astype(o_ref.dtype)


def _attention(proj3, bias, nb, d_ssm):
    bsz, s, _ = proj3.shape
    n_gate = bias.shape[2]
    n_heads = n_gate // nb
    d_attn = n_heads * HEAD_DIM
    q0 = d_ssm // HEAD_DIM
    k0 = q0 + n_heads
    v0 = k0 + n_heads
    assert ATTN_Q_BLOCKS == 2 and nb % ATTN_Q_BLOCKS == 0
    assert n_heads % ATTN_HEADS == 0 and q0 % ATTN_HEADS == 0
    tq = ATTN_Q_BLOCKS * MOBA_BLOCK
    hw = ATTN_HEADS * HEAD_DIM
    return pl.pallas_call(
        functools.partial(_attn_kernel, nb=nb),
        grid=(bsz, n_heads // ATTN_HEADS, s // tq),
        in_specs=[
            pl.BlockSpec((1, tq, hw), lambda b, h, i: (b, i, q0 // ATTN_HEADS + h)),
            pl.BlockSpec((1, s, hw), lambda b, h, i: (b, 0, k0 // ATTN_HEADS + h)),
            pl.BlockSpec((1, s, hw), lambda b, h, i: (b, 0, v0 // ATTN_HEADS + h)),
            pl.BlockSpec((1, tq, n_gate), lambda b, h, i: (b, i, 0)),
        ],
        out_specs=pl.BlockSpec((1, tq, hw), lambda b, h, i: (b, i, h)),
        out_shape=jax.ShapeDtypeStruct((bsz, s, d_attn), BF16),
        scratch_shapes=[
            pltpu.VMEM((ATTN_HEADS, tq, HEAD_DIM + n_gate), BF16),
            pltpu.VMEM((2, ATTN_HEADS, tq, MOBA_BLOCK), F32),
            pltpu.VMEM((ATTN_HEADS, tq, LANES), F32),
            pltpu.VMEM((ATTN_HEADS, tq, 2 * HEAD_DIM), F32),
        ],
        compiler_params=_params(3),
        name="moba_attn",
    )(proj3, proj3, proj3, bias)


def _ssm_tables(lam_re, lam_im, log_dt, b_re, b_im, c_re, c_im, d_skip):
    t_len = SSM_CHUNK
    n_groups, n_state = lam_re.shape
    n_ch = SSM_GROUP
    assert 2 * n_state == LANES
    w = t_len * n_ch
    dt = jnp.exp(log_dt)[:, None]
    ar = lam_re * dt
    ai = lam_im * dt
    steps = jnp.arange(t_len + 1, dtype=F32)[None, :, None]
    mag = jnp.exp(ar[:, None, :] * steps)
    pw_re = mag * jnp.cos(ai[:, None, :] * steps)
    pw_im = mag * jnp.sin(ai[:, None, :] * steps)
    e1 = jnp.expm1(ar)
    sh = jnp.sin(0.5 * ai)
    num_re = e1 * jnp.cos(ai) - 2.0 * sh * sh
    num_im = (e1 + 1.0) * jnp.sin(ai)
    den = lam_re * lam_re + lam_im * lam_im
    coef_re = (num_re * lam_re + num_im * lam_im) / den
    coef_im = (num_im * lam_re - num_re * lam_im) / den
    bb_re = coef_re[..., None] * b_re - coef_im[..., None] * b_im
    bb_im = coef_re[..., None] * b_im + coef_im[..., None] * b_re
    cp_re = c_re[:, None] * pw_re[:, :, None, :] - c_im[:, None] * pw_im[:, :, None, :]
    cp_im = c_re[:, None] * pw_im[:, :, None, :] + c_im[:, None] * pw_re[:, :, None, :]
    cpow = jnp.concatenate([cp_re, -cp_im], axis=-1).reshape(n_groups, (t_len + 1) * n_ch, LANES)
    lane_of = (jnp.arange(n_groups) % SSM_LANE_GROUPS)[:, None] * n_ch + jnp.arange(n_ch)[None, :]
    place = (lane_of[:, :, None] == jnp.arange(LANES)[None, None, :]).astype(F32)
    bshift = jnp.einsum('gpm,gml->gpl', jnp.concatenate([bb_re, bb_im], axis=1), place, precision=_HI)
    rev_re = pw_re[:, :t_len][:, ::-1][:, :, None, :]
    rev_im = pw_im[:, :t_len][:, ::-1][:, :, None, :]
    bt_re = bb_re.transpose(0, 2, 1)[:, None]
    bt_im = bb_im.transpose(0, 2, 1)[:, None]
    pin = jnp.stack([rev_re * bt_re - rev_im * bt_im, rev_re * bt_im + rev_im * bt_re], axis=1)
    pad_p = LANES - n_state
    pin = jnp.pad(pin.reshape(n_groups, 2, w, n_state), ((0, 0), (0, 0), (0, 0), (0, pad_p)))
    adec = jnp.stack([pw_re[:, t_len], pw_im[:, t_len]], axis=1)[:, :, None, :]
    adec = jnp.pad(adec, ((0, 0), (0, 0), (0, 0), (0, pad_p)))
    return cpow, bshift, pin.astype(BF16), adec


def _ssm_kernel(x_ref, cpow_ref, bshift_ref, pin_ref, adec_ref, dvec_ref, o_ref,
                rs_ref, xall_ref, hre_ref, him_ref, kk_ref, wt_ref, wrev_ref, pbd_ref, qbd_ref,
                *, bsz, nc):
    t_len = SSM_CHUNK
    n_pairs = t_len // 2
    ng = SSM_LANE_GROUPS
    gw = SSM_GROUP
    n_state = LANES // 2
    n_chunks = bsz * nc
    wide = ng * LANES

    for g in range(ng):
        kk_ref[g] = jnp.dot(cpow_ref[g, 0:t_len * gw, :], bshift_ref[g], preferred_element_type=F32,
                            precision=_HI)
    row = lax.broadcasted_iota(jnp.int32, (LANES, LANES), 0)
    col = lax.broadcasted_iota(jnp.int32, (LANES, LANES), 1)
    wt_ref[0] = jnp.zeros((LANES, LANES), BF16)
    for l in range(t_len):
        tile = kk_ref[:, l * gw:(l + 1) * gw, :].reshape(LANES, LANES)
        if l == 0:
            tile = tile + jnp.where(row == col, dvec_ref[0], 0.0)
        wt_ref[l + 1] = tile.astype(BF16)
    pw = 2 * LANES
    for d in range(n_pairs):
        c0 = (n_pairs - 1 - d) * pw
        wrev_ref[0:LANES, c0:c0 + LANES] = wt_ref[2 * d + 1]
        wrev_ref[0:LANES, c0 + LANES:c0 + pw] = wt_ref[2 * d]
        wrev_ref[LANES:, c0:c0 + LANES] = wt_ref[2 * d + 2]
        wrev_ref[LANES:, c0 + LANES:c0 + pw] = wt_ref[2 * d + 1]

    rs_ref[...] = x_ref[...].astype(F32)
    for s in range(t_len):
        xall_ref[:, s * LANES:(s + 1) * LANES] = rs_ref[pl.ds(s, n_chunks, stride=t_len), :].astype(BF16)

    pbd_ref[...] = jnp.zeros(pbd_ref.shape, BF16)
    for k in range(t_len // SSM_STATE_POS):
        for sl in range(SSM_STATE_POS):
            s0 = (k * SSM_STATE_POS + sl) * gw
            for g in range(ng):
                rows = slice(sl * LANES + g * gw, sl * LANES + (g + 1) * gw)
                lanes = slice(g * LANES, (g + 1) * LANES)
                pbd_ref[0, rows, lanes] = pin_ref[g, 0, s0:s0 + gw, :]
                pbd_ref[1, rows, lanes] = pin_ref[g, 1, s0:s0 + gw, :]
        xk = xall_ref[:, k * SSM_STATE_POS * LANES:(k + 1) * SSM_STATE_POS * LANES]
        h_re = jnp.dot(xk, pbd_ref[0], preferred_element_type=F32)
        h_im = jnp.dot(xk, pbd_ref[1], preferred_element_type=F32)
        if k == 0:
            hre_ref[...] = h_re
            him_ref[...] = h_im
        else:
            hre_ref[...] += h_re
            him_ref[...] += h_im

    a_re = jnp.concatenate([adec_ref[g, 0] for g in range(ng)], axis=1)
    a_im = jnp.concatenate([adec_ref[g, 1] for g in range(ng)], axis=1)

    def step(c, carry):
        new = []
        for b in range(bsz):
            s_re, s_im = carry[b]
            r = b * nc + c
            loc_re = hre_ref[pl.ds(r, 1), :]
            loc_im = him_ref[pl.ds(r, 1), :]
            hre_ref[pl.ds(r, 1), :] = s_re
            him_ref[pl.ds(r, 1), :] = s_im
            new.append((a_re * s_re - a_im * s_im + loc_re,
                        a_re * s_im + a_im * s_re + loc_im))
        return tuple(new)

    zero = jnp.zeros((1, wide), F32)
    lax.fori_loop(0, nc, step, tuple((zero, zero) for _ in range(bsz)), unroll=2)

    h_in = (hre_ref[...] + pltpu.roll(him_ref[...], n_state, 1)).astype(BF16)
    qbd_ref[...] = jnp.zeros(qbd_ref.shape, BF16)
    for q in range(n_pairs):
        for tl in range(2):
            s0 = (2 * q + tl + 1) * gw
            for g in range(ng):
                qbd_ref[tl * LANES + g * gw:tl * LANES + (g + 1) * gw, g * LANES:(g + 1) * LANES] = (
                    cpow_ref[g, s0:s0 + gw, :].astype(BF16))
        y = lax.dot_general(h_in, qbd_ref[...], _NT, preferred_element_type=F32)
        y = y + lax.dot_general(xall_ref[:, 0:(q + 1) * pw], wrev_ref[:, (n_pairs - 1 - q) * pw:], _NT,
                                preferred_element_type=F32)
        y = jax.nn.gelu(y)
        rs_ref[pl.ds(2 * q, n_chunks, stride=t_len), :] = y[:, 0:LANES]
        rs_ref[pl.ds(2 * q + 1, n_chunks, stride=t_len), :] = y[:, LANES:]
    o_ref[...] = rs_ref[...].astype(o_ref.dtype)


def _ssm(proj2, tables, dvec, layer, bsz, d_ssm):
    cpow, bshift, pin, adec = tables
    rows = proj2.shape[0]
    nc = rows // bsz // SSM_CHUNK
    n_chunks = bsz * nc
    ng = SSM_LANE_GROUPS
    n_pairs = SSM_CHUNK // 2
    wide = ng * LANES
    cp_rows = cpow.shape[2]
    w = SSM_CHUNK * SSM_GROUP
    return pl.pallas_call(
        functools.partial(_ssm_kernel, bsz=bsz, nc=nc),
        grid=(d_ssm // LANES,),
        in_specs=[
            pl.BlockSpec((rows, LANES), lambda j: (0, j), pipeline_mode=pl.Buffered(1)),
            pl.BlockSpec((None, ng, cp_rows, LANES), lambda j: (layer, j, 0, 0)),
            pl.BlockSpec((None, ng, LANES, LANES), lambda j: (layer, j, 0, 0)),
            pl.BlockSpec((None, ng, 2, w, LANES), lambda j: (layer, j, 0, 0, 0)),
            pl.BlockSpec((None, ng, 2, 1, LANES), lambda j: (layer, j, 0, 0, 0)),
            pl.BlockSpec((None, 1, 1, LANES), lambda j: (layer, j, 0, 0)),
        ],
        out_specs=pl.BlockSpec((rows, LANES), lambda j: (0, j)),
        out_shape=jax.ShapeDtypeStruct((rows, d_ssm), BF16),
        scratch_shapes=[
            pltpu.VMEM((rows, LANES), F32),
            pltpu.VMEM((n_chunks, SSM_CHUNK * LANES), BF16),
            pltpu.VMEM((n_chunks, wide), F32),
            pltpu.VMEM((n_chunks, wide), F32),
            pltpu.VMEM((ng, w, LANES), F32),
            pltpu.VMEM((SSM_CHUNK + 1, LANES, LANES), BF16),
            pltpu.VMEM((2 * LANES, n_pairs * 2 * LANES), BF16),
            pltpu.VMEM((2, SSM_STATE_POS * LANES, wide), BF16),
            pltpu.VMEM((2 * LANES, wide), BF16),
        ],
        compiler_params=_params(1),
        name="s5_chunked",
    )(proj2, cpow, bshift, pin, adec, dvec)


def _merge_kernel(y_ref, att_ref, ga_ref, gb_ref, x_ref, wglu_ref, wus_ref, wua_ref, wout_ref,
                  g_ref, o_ref):
    y = y_ref[...]
    z = jnp.dot(y, wglu_ref[...], preferred_element_type=F32)
    s5 = (y.astype(F32) * jax.nn.sigmoid(z)).astype(BF16)
    ys = jnp.dot(s5, wus_ref[...], preferred_element_type=F32)
    ya = jnp.dot(att_ref[...], wua_ref[...], preferred_element_type=F32)
    m = (jax.nn.sigmoid(ga_ref[...].astype(F32)) * ys
         + jax.nn.sigmoid(gb_ref[...].astype(F32)) * ya)
    o = jnp.dot(m.astype(BF16), wout_ref[...], preferred_element_type=F32)
    o_ref[...] = x_ref[...] + _rmsnorm_rows(o, g_ref[...])


def _merge(y2, att2, proj2, x2, w_glu, w_us, w_ua, w_out, layer, g_post):
    n_rows, d = x2.shape
    d_ssm = y2.shape[1]
    d_attn = att2.shape[1]
    tm = min(ROW_TILE_MERGE, n_rows)
    ga_blk = (d_ssm + 3 * d_attn) // d

    def weight(wt):
        return pl.BlockSpec((None,) + wt.shape[1:], lambda i: (layer, 0, 0),
                            pipeline_mode=pl.Buffered(1))

    return pl.pallas_call(
        _merge_kernel,
        grid=(n_rows // tm,),
        in_specs=[
            pl.BlockSpec((tm, d_ssm), lambda i: (i, 0)),
            pl.BlockSpec((tm, d_attn), lambda i: (i, 0)),
            pl.BlockSpec((tm, d), lambda i: (i, ga_blk)),
            pl.BlockSpec((tm, d), lambda i: (i, ga_blk + 1)),
            pl.BlockSpec((tm, d), lambda i: (i, 0)),
            weight(w_glu), weight(w_us), weight(w_ua), weight(w_out),
            pl.BlockSpec((1, d), lambda i: (0, 0)),
        ],
        out_specs=pl.BlockSpec((tm, d), lambda i: (i, 0)),
        out_shape=jax.ShapeDtypeStruct((n_rows, d), F32),
        compiler_params=_params(1),
        name="merge",
    )(y2, att2, proj2, proj2, x2, w_glu, w_us, w_ua, w_out, g_post)


def _ffn_kernel(x_ref, gpre_ref, wa_ref, wv_ref, cwa_ref, cwv_ref, cba_ref, cbv_ref, wd_ref,
                gpost_ref, o_ref, h_ref, acc_ref, za_ref, zv_ref, halo_a_ref, halo_v_ref,
                *, tiles_per_seq):
    i = pl.program_id(0)
    c = pl.program_id(1)
    tm = x_ref.shape[0]

    @pl.when(c == 0)
    def _():
        _rmsnorm_into(h_ref, x_ref, gpre_ref)
        acc_ref[...] = jnp.zeros_like(acc_ref)

    @pl.when((i % tiles_per_seq) == 0)
    def _():
        halo_a_ref[c] = jnp.zeros(halo_a_ref.shape[1:], F32)
        halo_v_ref[c] = jnp.zeros(halo_v_ref.shape[1:], F32)

    za_ref[0:SUBLANES, :] = halo_a_ref[c]
    zv_ref[0:SUBLANES, :] = halo_v_ref[c]
    cwa, cwv = cwa_ref[...], cwv_ref[...]
    cba, cbv = cba_ref[...], cbv_ref[...]

    def up(r0):
        h = h_ref[r0:r0 + FFN_STRIP, :]
        out_rows = slice(SUBLANES + r0, SUBLANES + r0 + FFN_STRIP)
        za_ref[out_rows, :] = jnp.dot(h, wa_ref[...], preferred_element_type=F32)
        zv_ref[out_rows, :] = jnp.dot(h, wv_ref[...], preferred_element_type=F32)

    def conv_strip(z_ref, cw, bias, r0):
        ext = z_ref[r0:r0 + FFN_STRIP + SUBLANES, :]
        z1 = pltpu.roll(ext, 1, 0)[SUBLANES:, :]
        z2 = pltpu.roll(ext, 2, 0)[SUBLANES:, :]
        return cw[0:1, :] * z2 + cw[1:2, :] * z1 + cw[2:3, :] * ext[SUBLANES:, :] + bias

    def finish(r0):
        a = conv_strip(za_ref, cwa, cba, r0)
        v = conv_strip(zv_ref, cwv, cbv, r0)
        gated = (a * jax.nn.sigmoid(a) * v).astype(BF16)
        acc_ref[r0:r0 + FFN_STRIP, :] += jnp.dot(gated, wd_ref[...], preferred_element_type=F32)

    strips = list(range(0, tm, FFN_STRIP))
    up(strips[0])
    for k, r0 in enumerate(strips):
        if k + 1 < len(strips):
            up(strips[k + 1])
        finish(r0)

    halo_a_ref[c] = za_ref[tm:tm + SUBLANES, :]
    halo_v_ref[c] = zv_ref[tm:tm + SUBLANES, :]

    @pl.when(c == pl.num_programs(1) - 1)
    def _():
        o_ref[...] = x_ref[...] + _rmsnorm_rows(acc_ref[...], gpost_ref[...])


def _ffn(x2, g_pre, w_up, conv_w, conv_b, w_down, layer, g_post, seq_len):
    n_rows, d = x2.shape
    d_ff = w_down.shape[1]
    tm = min(ROW_TILE_FFN, seq_len)
    tf = COL_TILE_FFN
    nff = d_ff // tf
    assert tm % FFN_STRIP == 0
    const = lambda i, c: (0, 0)
    zbuf = pltpu.VMEM((SUBLANES + tm, tf), F32)
    return pl.pallas_call(
        functools.partial(_ffn_kernel, tiles_per_seq=seq_len // tm),
        grid=(n_rows // tm, nff),
        in_specs=[
            pl.BlockSpec((tm, d), lambda i, c: (i, 0)),
            pl.BlockSpec((1, d), const),
            pl.BlockSpec((None, d, tf), lambda i, c: (layer, 0, c)),
            pl.BlockSpec((None, d, tf), lambda i, c: (layer, 0, nff + c)),
            pl.BlockSpec((None, CONV_W, tf), lambda i, c: (layer, 0, c)),
            pl.BlockSpec((None, CONV_W, tf), lambda i, c: (layer, 0, nff + c)),
            pl.BlockSpec((None, 1, tf), lambda i, c: (layer, 0, c)),
            pl.BlockSpec((None, 1, tf), lambda i, c: (layer, 0, nff + c)),
            pl.BlockSpec((None, tf, d), lambda i, c: (layer, c, 0)),
            pl.BlockSpec((1, d), const),
        ],
        out_specs=pl.BlockSpec((tm, d), lambda i, c: (i, 0)),
        out_shape=jax.ShapeDtypeStruct((n_rows, d), F32),
        scratch_shapes=[
            pltpu.VMEM((tm, d), BF16),
            pltpu.VMEM((tm, d), F32),
            zbuf, zbuf,
            pltpu.VMEM((nff, SUBLANES, tf), F32),
            pltpu.VMEM((nff, SUBLANES, tf), F32),
        ],
        compiler_params=_params(2),
        name="convglu_ffn",
    )(x2, g_pre, w_up, w_up, conv_w, conv_w, conv_b, conv_b, w_down, g_post)


def kernel(x, g_pre_mix, w_in, lam_re, lam_im, log_dt, b_re, b_im, c_re, c_im, d_skip, w_glu,
           w_up_ssm, w_up_attn, w_out, g_post_mix, g_pre_ffn, w_ffn_up, conv_w, conv_b,
           w_ffn_down, g_post_ffn):
    bsz, seq_len, d = x.shape
    depth = w_in.shape[0]
    d_ssm = w_glu.shape[1]
    d_attn = w_up_attn.shape[1]
    n_heads = d_attn // HEAD_DIM
    n_groups = d_ssm // SSM_GROUP
    assert seq_len % (ATTN_Q_BLOCKS * MOBA_BLOCK) == 0 and seq_len % SSM_CHUNK == 0
    assert d_ssm % d_attn == 0
    nb = seq_len // MOBA_BLOCK
    nc = seq_len // SSM_CHUNK
    n_rows = bsz * seq_len
    row = lambda v: v.reshape(1, -1)

    w_in_bf = w_in.astype(BF16)
    w_glu_bf = w_glu.astype(BF16)
    w_us_bf = w_up_ssm.astype(BF16)
    w_ua_bf = w_up_attn.astype(BF16)
    w_out_bf = w_out.astype(BF16)
    w_fu_bf = w_ffn_up.astype(BF16)
    w_fd_bf = w_ffn_down.astype(BF16)
    conv_b3 = conv_b[:, None, :]
    tables = jax.vmap(_ssm_tables)(lam_re, lam_im, log_dt, b_re, b_im, c_re, c_im, d_skip)
    dvec = d_skip.reshape(depth, d_ssm // LANES, 1, LANES)
    col_scale = jnp.ones((w_in.shape[2],), F32).at[d_ssm:d_ssm + d_attn].set(1.0 / math.sqrt(HEAD_DIM))

    x2 = x.reshape(n_rows, d)
    for l in range(depth):
        proj2 = _inproj(x2, row(g_pre_mix[l]), w_in_bf, l, row(col_scale))

        hbar = _blockmean(x2, row(g_pre_mix[l])).reshape(bsz, 1, nb, d)
        hbar_t = jnp.broadcast_to(hbar, (bsz, n_heads, nb, d)).reshape(bsz * n_heads * nb, d)
        kbd = _kmean(hbar_t, w_in, l, nb, d_ssm, d_attn).reshape(bsz, n_heads * nb, d_attn)
        z = _gatevec(w_in, l, kbd, d_ssm)
        bias = _select(x2.reshape(bsz, seq_len, d), row(g_pre_mix[l]), z, nb)
        att = _attention(proj2.reshape(bsz, seq_len, -1), bias, nb, d_ssm)

        y2 = _ssm(proj2, tables, dvec, l, bsz, d_ssm)

        x2 = _merge(y2, att.reshape(n_rows, d_attn), proj2, x2, w_glu_bf, w_us_bf, w_ua_bf,
                    w_out_bf, l, row(g_post_mix[l]))
        x2 = _ffn(x2, row(g_pre_ffn[l]), w_fu_bf, conv_w, conv_b3, w_fd_bf, l,
                  row(g_post_ffn[l]), seq_len)
    return x2.reshape(bsz, seq_len, d)
```

```python
import functools
import math

import jax
import jax.numpy as jnp
from jax import lax
from jax.experimental import pallas as pl
from jax.experimental.pallas import tpu as pltpu

F32 = jnp.float32
BF16 = jnp.bfloat16

EPS = 1e-6
NEG = -1e30
SSM_GROUP = 16
HEAD_DIM = 128
MOBA_BLOCK = 256
MOBA_TOPK = 3
CONV_W = 3

LANES = 128
SUBLANES = 8
VMEM_LIMIT_BYTES = 56 * 1024 * 1024

SSM_CHUNK = 32
SSM_LANE_GROUPS = LANES // SSM_GROUP
SSM_STATE_POS = 8
ROW_TILE_PROJ = 2048
COL_TILE_PROJ = 1024
ROW_TILE_GATE = 512
ROW_TILE_MERGE = 512
ROW_TILE_FFN = 1024
COL_TILE_FFN = 512
FFN_STRIP = 256
K_TILE_GATE = 512
NORM_ROWS = 128
ATTN_Q_BLOCKS = 2
ATTN_HEADS = 2

_NT = (((1,), (1,)), ((), ()))
_HI = lax.Precision.HIGHEST


def _params(n_axes):
    return pltpu.CompilerParams(
        dimension_semantics=("arbitrary",) * n_axes,
        vmem_limit_bytes=VMEM_LIMIT_BYTES,
    )


def _rmsnorm_rows(x, g):
    ms = jnp.mean(x * x, axis=-1, keepdims=True)
    return x * lax.rsqrt(ms + EPS) * g


def _rmsnorm_into(h_ref, x_ref, g_ref):
    step_rows = min(NORM_ROWS, x_ref.shape[0])

    def step(r, carry):
        r0 = pl.multiple_of(r * step_rows, step_rows)
        h_ref[pl.ds(r0, step_rows), :] = _rmsnorm_rows(
            x_ref[pl.ds(r0, step_rows), :], g_ref[...]).astype(h_ref.dtype)
        return carry

    lax.fori_loop(0, x_ref.shape[0] // step_rows, step, 0)


def _inproj_kernel(x_ref, g_ref, w_ref, cs_ref, o_ref, h_ref):
    @pl.when(pl.program_id(1) == 0)
    def _():
        _rmsnorm_into(h_ref, x_ref, g_ref)

    acc = jnp.dot(h_ref[...], w_ref[...], preferred_element_type=F32)
    o_ref[...] = (acc * cs_ref[...]).astype(o_ref.dtype)


def _col_blocked(w_bf, tn):
    depth, d, n = w_bf.shape
    return w_bf.reshape(depth, d, n // tn, tn).transpose(0, 2, 1, 3)


def _inproj(x2, g, w_blk, layer, col_scale):
    n_rows, d = x2.shape
    n_col, tn = w_blk.shape[1], w_blk.shape[3]
    d_in = n_col * tn
    tm = min(ROW_TILE_PROJ, n_rows)
    return pl.pallas_call(
        _inproj_kernel,
        grid=(n_rows // tm, n_col),
        in_specs=[
            pl.BlockSpec((tm, d), lambda i, j: (i, 0), pipeline_mode=pl.Buffered(1)),
            pl.BlockSpec((1, d), lambda i, j: (0, 0)),
            pl.BlockSpec((None, None, d, tn), lambda i, j: (layer, j, 0, 0)),
            pl.BlockSpec((1, tn), lambda i, j: (0, j)),
        ],
        out_specs=pl.BlockSpec((tm, tn), lambda i, j: (i, j)),
        out_shape=jax.ShapeDtypeStruct((n_rows, d_in), BF16),
        scratch_shapes=[pltpu.VMEM((tm, d), BF16)],
        compiler_params=_params(2),
        name="inproj",
    )(x2, g, w_blk, col_scale)


def _blockmean_kernel(x_ref, g_ref, o_ref):
    h = _rmsnorm_rows(x_ref[...], g_ref[...])
    o_ref[0] = jnp.mean(h, axis=0, keepdims=True)


def _blockmean(x2, g):
    n_rows, d = x2.shape
    n_blocks = n_rows // MOBA_BLOCK
    return pl.pallas_call(
        _blockmean_kernel,
        grid=(n_blocks,),
        in_specs=[
            pl.BlockSpec((MOBA_BLOCK, d), lambda i: (i, 0)),
            pl.BlockSpec((1, d), lambda i: (0, 0)),
        ],
        out_specs=pl.BlockSpec((1, 1, d), lambda i: (i, 0, 0)),
        out_shape=jax.ShapeDtypeStruct((n_blocks, 1, d), F32),
        compiler_params=_params(1),
        name="blockmean",
    )(x2, g)


def _kmean_kernel(hb_ref, wk_ref, o_ref, *, nb):
    @pl.when(pl.program_id(0) == 0)
    def _():
        o_ref[...] = jnp.zeros_like(o_ref)

    o_ref[...] += jnp.dot(hb_ref[...], wk_ref[...], preferred_element_type=F32, precision=_HI)

    @pl.when(pl.program_id(0) == pl.num_programs(0) - 1)
    def _():
        r = lax.broadcasted_iota(jnp.int32, o_ref.shape, 0)
        c = lax.broadcasted_iota(jnp.int32, o_ref.shape, 1)
        n_gate = (o_ref.shape[1] // HEAD_DIM) * nb
        keep = (c // HEAD_DIM) == ((r % n_gate) // nb)
        o_ref[...] = jnp.where(keep, o_ref[...], 0.0)


def _kmean(hbar_t, w_in, layer, nb, d_ssm, d_attn):
    rows, d = hbar_t.shape
    k_col_block = (d_ssm + d_attn) // d_attn
    tk = K_TILE_GATE
    return pl.pallas_call(
        functools.partial(_kmean_kernel, nb=nb),
        grid=(d // tk,),
        in_specs=[
            pl.BlockSpec((rows, tk), lambda kk: (0, kk)),
            pl.BlockSpec((None, tk, d_attn), lambda kk: (layer, kk, k_col_block)),
        ],
        out_specs=pl.BlockSpec((rows, d_attn), lambda kk: (0, 0)),
        out_shape=jax.ShapeDtypeStruct((rows, d_attn), F32),
        compiler_params=_params(1),
        name="kmean",
    )(hbar_t, w_in)


def _gatevec_kernel(wq_ref, kbd_ref, o_ref):
    o_ref[0] = lax.dot_general(wq_ref[...], kbd_ref[0], _NT, preferred_element_type=F32,
                               precision=_HI)


def _gatevec(w_in, layer, kbd, d_ssm):
    bsz, n_gate, d_attn = kbd.shape
    d = w_in.shape[1]
    q_col_block = d_ssm // d_attn
    tk = K_TILE_GATE
    return pl.pallas_call(
        _gatevec_kernel,
        grid=(bsz, d // tk),
        in_specs=[
            pl.BlockSpec((None, tk, d_attn), lambda b, kk: (layer, kk, q_col_block)),
            pl.BlockSpec((1, n_gate, d_attn), lambda b, kk: (b, 0, 0)),
        ],
        out_specs=pl.BlockSpec((1, tk, n_gate), lambda b, kk: (b, kk, 0)),
        out_shape=jax.ShapeDtypeStruct((bsz, d, n_gate), F32),
        compiler_params=_params(2),
        name="gatevec",
    )(w_in, kbd)


def _select_kernel(x_ref, g_ref, z_ref, o_ref, *, nb, tq):
    h = _rmsnorm_rows(x_ref[0], g_ref[...])
    gate = jnp.dot(h, z_ref[0], preferred_element_type=F32, precision=_HI)
    n_gate = gate.shape[1]
    row = lax.broadcasted_iota(jnp.int32, gate.shape, 0) + pl.program_id(1) * tq
    qblk = row // MOBA_BLOCK
    j = lax.broadcasted_iota(jnp.int32, gate.shape, 1) % nb
    past = j < qblk
    gate = jnp.where(past, gate, NEG)
    cnt = jnp.zeros(gate.shape, F32)
    for r in range(1, nb):
        lower = pltpu.roll(gate, r, 1)
        upper = pltpu.roll(gate, (r - nb) % n_gate, 1)
        has_lower = j >= r
        other = jnp.where(has_lower, lower, upper)
        wins_tie = jnp.where(other >= gate, 1.0, 0.0)
        wins_strict = jnp.where(other > gate, 1.0, 0.0)
        cnt = cnt + jnp.where(has_lower, wins_tie, wins_strict)
    keep = jnp.logical_or(jnp.logical_and(past, cnt < float(MOBA_TOPK)), j == qblk)
    o_ref[0] = jnp.where(keep, 0.0, NEG)


def _select(x3, g, z, nb):
    bsz, s, d = x3.shape
    n_gate = z.shape[2]
    tq = min(ROW_TILE_GATE, s)
    return pl.pallas_call(
        functools.partial(_select_kernel, nb=nb, tq=tq),
        grid=(bsz, s // tq),
        in_specs=[
            pl.BlockSpec((1, tq, d), lambda b, i: (b, i, 0)),
            pl.BlockSpec((1, d), lambda b, i: (0, 0)),
            pl.BlockSpec((1, d, n_gate), lambda b, i: (b, 0, 0)),
        ],
        out_specs=pl.BlockSpec((1, tq, n_gate), lambda b, i: (b, i, 0)),
        out_shape=jax.ShapeDtypeStruct((bsz, s, n_gate), F32),
        compiler_params=_params(2),
        name="select",
    )(x3, g, z)


def _attn_kernel(q_ref, k_ref, v_ref, bias_ref, o_ref, qa_ref, s_ref, m_ref, acc_ref, *, nb):
    hp = pl.program_id(1)
    i = pl.program_id(2)
    blk = MOBA_BLOCK
    tq = q_ref.shape[1]
    n_gate = bias_ref.shape[2]
    heads = range(ATTN_HEADS)
    mask_rows = bias_ref[0].astype(BF16)
    for hh in heads:
        qa_ref[hh, :, :HEAD_DIM] = q_ref[0, :, hh * HEAD_DIM:(hh + 1) * HEAD_DIM]
        qa_ref[hh, :, HEAD_DIM:] = mask_rows
    m_ref[...] = jnp.full(m_ref.shape, -jnp.inf, F32)
    acc_ref[...] = jnp.zeros(acc_ref.shape, F32)
    lane = lax.broadcasted_iota(jnp.int32, (blk, n_gate), 1)
    ones = jnp.ones((blk, HEAD_DIM), BF16)

    def scores(slot, hh, jb):
        st = pl.multiple_of(jb * blk, blk)
        col = (hp * ATTN_HEADS + hh) * nb + jb
        onehot = jnp.where(lane == col, 1.0, 0.0).astype(BF16)
        kj = k_ref[0, pl.ds(st, blk), hh * HEAD_DIM:(hh + 1) * HEAD_DIM]
        s_ref[slot, hh] = lax.dot_general(qa_ref[hh], jnp.concatenate([kj, onehot], axis=1), _NT,
                                          preferred_element_type=F32)

    def update(slot, hh, jb, causal):
        s = s_ref[slot, hh]
        if causal:
            qpos = lax.broadcasted_iota(jnp.int32, s.shape, 0) + i * tq
            kpos = lax.broadcasted_iota(jnp.int32, s.shape, 1) + jb * blk
            s = jnp.where(kpos <= qpos, s, NEG)
        st = pl.multiple_of(jb * blk, blk)
        m_old = m_ref[hh]
        m_new = jnp.maximum(m_old, jnp.max(s, axis=-1, keepdims=True))
        alpha = jnp.exp(m_old - m_new)
        p = jnp.exp(s - jnp.concatenate([m_new] * (blk // LANES), axis=1))
        vj = v_ref[0, pl.ds(st, blk), hh * HEAD_DIM:(hh + 1) * HEAD_DIM]
        acc_ref[hh] = jnp.concatenate([alpha, alpha], axis=1) * acc_ref[hh] + jnp.dot(
            p.astype(BF16), jnp.concatenate([vj, ones], axis=1), preferred_element_type=F32)
        m_ref[hh] = m_new

    for hh in heads:
        scores(0, hh, 0)

    def pair(jj, carry):
        for hh in heads:
            scores(1, hh, 2 * jj + 1)
        for hh in heads:
            update(0, hh, 2 * jj, False)
        for hh in heads:
            scores(0, hh, 2 * jj + 2)
        for hh in heads:
            update(1, hh, 2 * jj + 1, False)
        return carry

    lax.fori_loop(0, i, pair, 0)
    for hh in heads:
        scores(1, hh, 2 * i + 1)
    for hh in heads:
        update(0, hh, 2 * i, True)
    for hh in heads:
        update(1, hh, 2 * i + 1, True)
    for hh in heads:
        o_ref[0, :, hh * HEAD_DIM:(hh + 1) * HEAD_DIM] = (
            acc_ref[hh, :, :HEAD_DIM] / acc_ref[hh, :, HEAD_DIM:]).astype(o_ref.dtype)


def _attention(proj3, bias, nb, d_ssm):
    bsz, s, _ = proj3.shape
    n_gate = bias.shape[2]
    n_heads = n_gate // nb
    d_attn = n_heads * HEAD_DIM
    q0 = d_ssm // HEAD_DIM
    k0 = q0 + n_heads
    v0 = k0 + n_heads
    assert ATTN_Q_BLOCKS == 2 and nb % ATTN_Q_BLOCKS == 0
    assert n_heads % ATTN_HEADS == 0 and q0 % ATTN_HEADS == 0
    tq = ATTN_Q_BLOCKS * MOBA_BLOCK
    hw = ATTN_HEADS * HEAD_DIM
    return pl.pallas_call(
        functools.partial(_attn_kernel, nb=nb),
        grid=(bsz, n_heads // ATTN_HEADS, s // tq),
        in_specs=[
            pl.BlockSpec((1, tq, hw), lambda b, h, i: (b, i, q0 // ATTN_HEADS + h)),
            pl.BlockSpec((1, s, hw), lambda b, h, i: (b, 0, k0 // ATTN_HEADS + h)),
            pl.BlockSpec((1, s, hw), lambda b, h, i: (b, 0, v0 // ATTN_HEADS + h)),
            pl.BlockSpec((1, tq, n_gate), lambda b, h, i: (b, i, 0)),
        ],
        out_specs=pl.BlockSpec((1, tq, hw), lambda b, h, i: (b, i, h)),
        out_shape=jax.ShapeDtypeStruct((bsz, s, d_attn), BF16),
        scratch_shapes=[
            pltpu.VMEM((ATTN_HEADS, tq, HEAD_DIM + n_gate), BF16),
            pltpu.VMEM((2, ATTN_HEADS, tq, MOBA_BLOCK), F32),
            pltpu.VMEM((ATTN_HEADS, tq, LANES), F32),
            pltpu.VMEM((ATTN_HEADS, tq, 2 * HEAD_DIM), F32),
        ],
        compiler_params=_params(3),
        name="moba_attn",
    )(proj3, proj3, proj3, bias)


def _ssm_tables(lam_re, lam_im, log_dt, b_re, b_im, c_re, c_im, d_skip):
    t_len = SSM_CHUNK
    n_groups, n_state = lam_re.shape
    n_ch = SSM_GROUP
    assert 2 * n_state == LANES
    w = t_len * n_ch
    dt = jnp.exp(log_dt)[:, None]
    ar = lam_re * dt
    ai = lam_im * dt
    steps = jnp.arange(t_len + 1, dtype=F32)[None, :, None]
    mag = jnp.exp(ar[:, None, :] * steps)
    pw_re = mag * jnp.cos(ai[:, None, :] * steps)
    pw_im = mag * jnp.sin(ai[:, None, :] * steps)
    e1 = jnp.expm1(ar)
    sh = jnp.sin(0.5 * ai)
    num_re = e1 * jnp.cos(ai) - 2.0 * sh * sh
    num_im = (e1 + 1.0) * jnp.sin(ai)
    den = lam_re * lam_re + lam_im * lam_im
    coef_re = (num_re * lam_re + num_im * lam_im) / den
    coef_im = (num_im * lam_re - num_re * lam_im) / den
    bb_re = coef_re[..., None] * b_re - coef_im[..., None] * b_im
    bb_im = coef_re[..., None] * b_im + coef_im[..., None] * b_re
    cp_re = c_re[:, None] * pw_re[:, :, None, :] - c_im[:, None] * pw_im[:, :, None, :]
    cp_im = c_re[:, None] * pw_im[:, :, None, :] + c_im[:, None] * pw_re[:, :, None, :]
    cpow = jnp.concatenate([cp_re, -cp_im], axis=-1).reshape(n_groups, (t_len + 1) * n_ch, LANES)
    lane_of = (jnp.arange(n_groups) % SSM_LANE_GROUPS)[:, None] * n_ch + jnp.arange(n_ch)[None, :]
    place = (lane_of[:, :, None] == jnp.arange(LANES)[None, None, :]).astype(F32)
    bshift = jnp.einsum('gpm,gml->gpl', jnp.concatenate([bb_re, bb_im], axis=1), place, precision=_HI)
    rev_re = pw_re[:, :t_len][:, ::-1][:, :, None, :]
    rev_im = pw_im[:, :t_len][:, ::-1][:, :, None, :]
    bt_re = bb_re.transpose(0, 2, 1)[:, None]
    bt_im = bb_im.transpose(0, 2, 1)[:, None]
    pin = jnp.stack([rev_re * bt_re - rev_im * bt_im, rev_re * bt_im + rev_im * bt_re], axis=1)
    pad_p = LANES - n_state
    pin = jnp.pad(pin.reshape(n_groups, 2, w, n_state), ((0, 0), (0, 0), (0, 0), (0, pad_p)))
    adec = jnp.stack([pw_re[:, t_len], pw_im[:, t_len]], axis=1)[:, :, None, :]
    adec = jnp.pad(adec, ((0, 0), (0, 0), (0, 0), (0, pad_p)))
    return cpow, bshift, pin.astype(BF16), adec


def _ssm_kernel(x_ref, cpow_ref, bshift_ref, pin_ref, adec_ref, dvec_ref, o_ref,
                rs_ref, xall_ref, hre_ref, him_ref, kk_ref, wt_ref, wrev_ref, pbd_ref, qbd_ref,
                *, bsz, nc):
    t_len = SSM_CHUNK
    n_pairs = t_len // 2
    ng = SSM_LANE_GROUPS
    gw = SSM_GROUP
    n_state = LANES // 2
    n_chunks = bsz * nc
    wide = ng * LANES

    for g in range(ng):
        kk_ref[g] = jnp.dot(cpow_ref[g, 0:t_len * gw, :], bshift_ref[g], preferred_element_type=F32,
                            precision=_HI)
    row = lax.broadcasted_iota(jnp.int32, (LANES, LANES), 0)
    col = lax.broadcasted_iota(jnp.int32, (LANES, LANES), 1)
    wt_ref[0] = jnp.zeros((LANES, LANES), BF16)
    for l in range(t_len):
        tile = kk_ref[:, l * gw:(l + 1) * gw, :].reshape(LANES, LANES)
        if l == 0:
            tile = tile + jnp.where(row == col, dvec_ref[0], 0.0)
        wt_ref[l + 1] = tile.astype(BF16)
    pw = 2 * LANES
    for d in range(n_pairs):
        c0 = (n_pairs - 1 - d) * pw
        wrev_ref[0:LANES, c0:c0 + LANES] = wt_ref[2 * d + 1]
        wrev_ref[0:LANES, c0 + LANES:c0 + pw] = wt_ref[2 * d]
        wrev_ref[LANES:, c0:c0 + LANES] = wt_ref[2 * d + 2]
        wrev_ref[LANES:, c0 + LANES:c0 + pw] = wt_ref[2 * d + 1]

    rs_ref[...] = x_ref[...].astype(F32)
    for s in range(t_len):
        xall_ref[:, s * LANES:(s + 1) * LANES] = rs_ref[pl.ds(s, n_chunks, stride=t_len), :].astype(BF16)

    pbd_ref[...] = jnp.zeros(pbd_ref.shape, BF16)
    for k in range(t_len // SSM_STATE_POS):
        for sl in range(SSM_STATE_POS):
            s0 = (k * SSM_STATE_POS + sl) * gw
            for g in range(ng):
                rows = slice(sl * LANES + g * gw, sl * LANES + (g + 1) * gw)
                lanes = slice(g * LANES, (g + 1) * LANES)
                pbd_ref[0, rows, lanes] = pin_ref[g, 0, s0:s0 + gw, :]
                pbd_ref[1, rows, lanes] = pin_ref[g, 1, s0:s0 + gw, :]
        xk = xall_ref[:, k * SSM_STATE_POS * LANES:(k + 1) * SSM_STATE_POS * LANES]
        h_re = jnp.dot(xk, pbd_ref[0], preferred_element_type=F32)
        h_im = jnp.dot(xk, pbd_ref[1], preferred_element_type=F32)
        if k == 0:
            hre_ref[...] = h_re
            him_ref[...] = h_im
        else:
            hre_ref[...] += h_re
            him_ref[...] += h_im

    a_re = jnp.concatenate([adec_ref[g, 0] for g in range(ng)], axis=1)
    a_im = jnp.concatenate([adec_ref[g, 1] for g in range(ng)], axis=1)

    def step(c, carry):
        new = []
        for b in range(bsz):
            s_re, s_im = carry[b]
            r = b * nc + c
            loc_re = hre_ref[pl.ds(r, 1), :]
            loc_im = him_ref[pl.ds(r, 1), :]
            hre_ref[pl.ds(r, 1), :] = s_re
            him_ref[pl.ds(r, 1), :] = s_im
            new.append((a_re * s_re - a_im * s_im + loc_re,
                        a_re * s_im + a_im * s_re + loc_im))
        return tuple(new)

    zero = jnp.zeros((1, wide), F32)
    lax.fori_loop(0, nc, step, tuple((zero, zero) for _ in range(bsz)), unroll=2)

    h_in = (hre_ref[...] + pltpu.roll(him_ref[...], n_state, 1)).astype(BF16)
    qbd_ref[...] = jnp.zeros(qbd_ref.shape, BF16)
    for q in range(n_pairs):
        for tl in range(2):
            s0 = (2 * q + tl + 1) * gw
            for g in range(ng):
                qbd_ref[tl * LANES + g * gw:tl * LANES + (g + 1) * gw, g * LANES:(g + 1) * LANES] = (
                    cpow_ref[g, s0:s0 + gw, :].astype(BF16))
        y = lax.dot_general(h_in, qbd_ref[...], _NT, preferred_element_type=F32)
        y = y + lax.dot_general(xall_ref[:, 0:(q + 1) * pw], wrev_ref[:, (n_pairs - 1 - q) * pw:], _NT,
                                preferred_element_type=F32)
        y = jax.nn.gelu(y)
        rs_ref[pl.ds(2 * q, n_chunks, stride=t_len), :] = y[:, 0:LANES]
        rs_ref[pl.ds(2 * q + 1, n_chunks, stride=t_len), :] = y[:, LANES:]
    o_ref[...] = rs_ref[...].astype(o_ref.dtype)


def _ssm(proj2, tables, dvec, layer, bsz, d_ssm):
    cpow, bshift, pin, adec = tables
    rows = proj2.shape[0]
    nc = rows // bsz // SSM_CHUNK
    n_chunks = bsz * nc
    ng = SSM_LANE_GROUPS
    n_pairs = SSM_CHUNK // 2
    wide = ng * LANES
    cp_rows = cpow.shape[2]
    w = SSM_CHUNK * SSM_GROUP
    return pl.pallas_call(
        functools.partial(_ssm_kernel, bsz=bsz, nc=nc),
        grid=(d_ssm // LANES,),
        in_specs=[
            pl.BlockSpec((rows, LANES), lambda j: (0, j), pipeline_mode=pl.Buffered(1)),
            pl.BlockSpec((None, ng, cp_rows, LANES), lambda j: (layer, j, 0, 0)),
            pl.BlockSpec((None, ng, LANES, LANES), lambda j: (layer, j, 0, 0)),
            pl.BlockSpec((None, ng, 2, w, LANES), lambda j: (layer, j, 0, 0, 0)),
            pl.BlockSpec((None, ng, 2, 1, LANES), lambda j: (layer, j, 0, 0, 0)),
            pl.BlockSpec((None, 1, 1, LANES), lambda j: (layer, j, 0, 0)),
        ],
        out_specs=pl.BlockSpec((rows, LANES), lambda j: (0, j)),
        out_shape=jax.ShapeDtypeStruct((rows, d_ssm), BF16),
        scratch_shapes=[
            pltpu.VMEM((rows, LANES), F32),
            pltpu.VMEM((n_chunks, SSM_CHUNK * LANES), BF16),
            pltpu.VMEM((n_chunks, wide), F32),
            pltpu.VMEM((n_chunks, wide), F32),
            pltpu.VMEM((ng, w, LANES), F32),
            pltpu.VMEM((SSM_CHUNK + 1, LANES, LANES), BF16),
            pltpu.VMEM((2 * LANES, n_pairs * 2 * LANES), BF16),
            pltpu.VMEM((2, SSM_STATE_POS * LANES, wide), BF16),
            pltpu.VMEM((2 * LANES, wide), BF16),
        ],
        compiler_params=_params(1),
        name="s5_chunked",
    )(proj2, cpow, bshift, pin, adec, dvec)


def _merge_kernel(y_ref, att_ref, ga_ref, gb_ref, x_ref, wglu_ref, wus_ref, wua_ref, wout_ref,
                  g_ref, o_ref):
    y = y_ref[...]
    z = jnp.dot(y, wglu_ref[...], preferred_element_type=F32)
    s5 = (y.astype(F32) * jax.nn.sigmoid(z)).astype(BF16)
    ys = jnp.dot(s5, wus_ref[...], preferred_element_type=F32)
    ya = jnp.dot(att_ref[...], wua_ref[...], preferred_element_type=F32)
    m = (jax.nn.sigmoid(ga_ref[...].astype(F32)) * ys
         + jax.nn.sigmoid(gb_ref[...].astype(F32)) * ya)
    o = jnp.dot(m.astype(BF16), wout_ref[...], preferred_element_type=F32)
    o_ref[...] = x_ref[...] + _rmsnorm_rows(o, g_ref[...])


def _merge(y2, att2, proj2, x2, w_glu, w_us, w_ua, w_out, layer, g_post):
    n_rows, d = x2.shape
    d_ssm = y2.shape[1]
    d_attn = att2.shape[1]
    tm = min(ROW_TILE_MERGE, n_rows)
    ga_blk = (d_ssm + 3 * d_attn) // d

    def weight(wt):
        return pl.BlockSpec((None,) + wt.shape[1:], lambda i: (layer, 0, 0),
                            pipeline_mode=pl.Buffered(1))

    return pl.pallas_call(
        _merge_kernel,
        grid=(n_rows // tm,),
        in_specs=[
            pl.BlockSpec((tm, d_ssm), lambda i: (i, 0)),
            pl.BlockSpec((tm, d_attn), lambda i: (i, 0)),
            pl.BlockSpec((tm, d), lambda i: (i, ga_blk)),
            pl.BlockSpec((tm, d), lambda i: (i, ga_blk + 1)),
            pl.BlockSpec((tm, d), lambda i: (i, 0)),
            weight(w_glu), weight(w_us), weight(w_ua), weight(w_out),
            pl.BlockSpec((1, d), lambda i: (0, 0)),
        ],
        out_specs=pl.BlockSpec((tm, d), lambda i: (i, 0)),
        out_shape=jax.ShapeDtypeStruct((n_rows, d), F32),
        compiler_params=_params(1),
        name="merge",
    )(y2, att2, proj2, proj2, x2, w_glu, w_us, w_ua, w_out, g_post)


def _ffn_kernel(x_ref, gpre_ref, wa_ref, wv_ref, cwa_ref, cwv_ref, cba_ref, cbv_ref, wd_ref,
                gpost_ref, o_ref, h_ref, za_ref, zv_ref, halo_a_ref, halo_v_ref,
                *, tiles_per_seq):
    i = pl.program_id(0)
    c = pl.program_id(1)
    tm = x_ref.shape[0]

    @pl.when(c == 0)
    def _():
        _rmsnorm_into(h_ref, x_ref, gpre_ref)
        o_ref[...] = jnp.zeros_like(o_ref)

    @pl.when((i % tiles_per_seq) == 0)
    def _():
        halo_a_ref[c] = jnp.zeros(halo_a_ref.shape[1:], F32)
        halo_v_ref[c] = jnp.zeros(halo_v_ref.shape[1:], F32)

    za_ref[0:SUBLANES, :] = halo_a_ref[c]
    zv_ref[0:SUBLANES, :] = halo_v_ref[c]
    cwa, cwv = cwa_ref[...], cwv_ref[...]
    cba, cbv = cba_ref[...], cbv_ref[...]

    def up(r0):
        h = h_ref[r0:r0 + FFN_STRIP, :]
        out_rows = slice(SUBLANES + r0, SUBLANES + r0 + FFN_STRIP)
        za_ref[out_rows, :] = jnp.dot(h, wa_ref[...], preferred_element_type=F32)
        zv_ref[out_rows, :] = jnp.dot(h, wv_ref[...], preferred_element_type=F32)

    def conv_strip(z_ref, cw, bias, r0):
        ext = z_ref[r0:r0 + FFN_STRIP + SUBLANES, :]
        z1 = pltpu.roll(ext, 1, 0)[SUBLANES:, :]
        z2 = pltpu.roll(ext, 2, 0)[SUBLANES:, :]
        return cw[0:1, :] * z2 + cw[1:2, :] * z1 + cw[2:3, :] * ext[SUBLANES:, :] + bias

    def finish(r0):
        a = conv_strip(za_ref, cwa, cba, r0)
        v = conv_strip(zv_ref, cwv, cbv, r0)
        gated = (a * jax.nn.sigmoid(a) * v).astype(BF16)
        o_ref[r0:r0 + FFN_STRIP, :] += jnp.dot(gated, wd_ref[...], preferred_element_type=F32)

    strips = list(range(0, tm, FFN_STRIP))
    up(strips[0])
    for k, r0 in enumerate(strips):
        if k + 1 < len(strips):
            up(strips[k + 1])
        finish(r0)

    halo_a_ref[c] = za_ref[tm:tm + SUBLANES, :]
    halo_v_ref[c] = zv_ref[tm:tm + SUBLANES, :]

    @pl.when(c == pl.num_programs(1) - 1)
    def _():
        step_rows = min(NORM_ROWS, tm)

        def finalize(r, carry):
            rows = pl.ds(pl.multiple_of(r * step_rows, step_rows), step_rows)
            o_ref[rows, :] = x_ref[rows, :] + _rmsnorm_rows(o_ref[rows, :], gpost_ref[...])
            return carry

        lax.fori_loop(0, tm // step_rows, finalize, 0)


def _ffn(x2, g_pre, w_up_blk, conv_w, conv_b, w_down, layer, g_post, seq_len):
    n_rows, d = x2.shape
    d_ff = w_down.shape[1]
    tm = min(ROW_TILE_FFN, seq_len)
    tf = COL_TILE_FFN
    nff = d_ff // tf
    assert tm % FFN_STRIP == 0 and w_up_blk.shape[1] == 2 * nff and w_up_blk.shape[3] == tf
    const = lambda i, c: (0, 0)
    zbuf = pltpu.VMEM((SUBLANES + tm, tf), F32)
    return pl.pallas_call(
        functools.partial(_ffn_kernel, tiles_per_seq=seq_len // tm),
        grid=(n_rows // tm, nff),
        in_specs=[
            pl.BlockSpec((tm, d), lambda i, c: (i, 0), pipeline_mode=pl.Buffered(1)),
            pl.BlockSpec((1, d), const),
            pl.BlockSpec((None, None, d, tf), lambda i, c: (layer, c, 0, 0)),
            pl.BlockSpec((None, None, d, tf), lambda i, c: (layer, nff + c, 0, 0)),
            pl.BlockSpec((None, CONV_W, tf), lambda i, c: (layer, 0, c)),
            pl.BlockSpec((None, CONV_W, tf), lambda i, c: (layer, 0, nff + c)),
            pl.BlockSpec((None, 1, tf), lambda i, c: (layer, 0, c)),
            pl.BlockSpec((None, 1, tf), lambda i, c: (layer, 0, nff + c)),
            pl.BlockSpec((None, tf, d), lambda i, c: (layer, c, 0)),
            pl.BlockSpec((1, d), const),
        ],
        out_specs=pl.BlockSpec((tm, d), lambda i, c: (i, 0)),
        out_shape=jax.ShapeDtypeStruct((n_rows, d), F32),
        scratch_shapes=[
            pltpu.VMEM((tm, d), BF16),
            zbuf, zbuf,
            pltpu.VMEM((nff, SUBLANES, tf), F32),
            pltpu.VMEM((nff, SUBLANES, tf), F32),
        ],
        compiler_params=_params(2),
        name="convglu_ffn",
    )(x2, g_pre, w_up_blk, w_up_blk, conv_w, conv_w, conv_b, conv_b, w_down, g_post)


def kernel(x, g_pre_mix, w_in, lam_re, lam_im, log_dt, b_re, b_im, c_re, c_im, d_skip, w_glu,
           w_up_ssm, w_up_attn, w_out, g_post_mix, g_pre_ffn, w_ffn_up, conv_w, conv_b,
           w_ffn_down, g_post_ffn):
    bsz, seq_len, d = x.shape
    depth = w_in.shape[0]
    d_ssm = w_glu.shape[1]
    d_attn = w_up_attn.shape[1]
    n_heads = d_attn // HEAD_DIM
    n_groups = d_ssm // SSM_GROUP
    assert seq_len % (ATTN_Q_BLOCKS * MOBA_BLOCK) == 0 and seq_len % SSM_CHUNK == 0
    assert d_ssm % d_attn == 0
    nb = seq_len // MOBA_BLOCK
    nc = seq_len // SSM_CHUNK
    n_rows = bsz * seq_len
    row = lambda v: v.reshape(1, -1)

    w_in_bf = _col_blocked(w_in.astype(BF16), COL_TILE_PROJ)
    w_glu_bf = w_glu.astype(BF16)
    w_us_bf = w_up_ssm.astype(BF16)
    w_ua_bf = w_up_attn.astype(BF16)
    w_out_bf = w_out.astype(BF16)
    w_fu_bf = _col_blocked(w_ffn_up.astype(BF16), COL_TILE_FFN)
    w_fd_bf = w_ffn_down.astype(BF16)
    conv_b3 = conv_b[:, None, :]
    tables = jax.vmap(_ssm_tables)(lam_re, lam_im, log_dt, b_re, b_im, c_re, c_im, d_skip)
    dvec = d_skip.reshape(depth, d_ssm // LANES, 1, LANES)
    col_scale = jnp.ones((w_in.shape[2],), F32).at[d_ssm:d_ssm + d_attn].set(1.0 / math.sqrt(HEAD_DIM))

    x2 = x.reshape(n_rows, d)
    for l in range(depth):
        proj2 = _inproj(x2, row(g_pre_mix[l]), w_in_bf, l, row(col_scale))

        hbar = _blockmean(x2, row(g_pre_mix[l])).reshape(bsz, 1, nb, d)
        hbar_t = jnp.broadcast_to(hbar, (bsz, n_heads, nb, d)).reshape(bsz * n_heads * nb, d)
        kbd = _kmean(hbar_t, w_in, l, nb, d_ssm, d_attn).reshape(bsz, n_heads * nb, d_attn)
        z = _gatevec(w_in, l, kbd, d_ssm)
        bias = _select(x2.reshape(bsz, seq_len, d), row(g_pre_mix[l]), z, nb)
        att = _attention(proj2.reshape(bsz, seq_len, -1), bias, nb, d_ssm)

        y2 = _ssm(proj2, tables, dvec, l, bsz, d_ssm)

        x2 = _merge(y2, att.reshape(n_rows, d_attn), proj2, x2, w_glu_bf, w_us_bf, w_ua_bf,
                    w_out_bf, l, row(g_post_mix[l]))
        x2 = _ffn(x2, row(g_pre_ffn[l]), w_fu_bf, conv_w, conv_b3, w_fd_bf, l,
                  row(g_post_ffn[l]), seq_len)
    return x2.reshape(bsz, seq_len, d)
```

```python
import functools
import math

import jax
import jax.numpy as jnp
from jax import lax
from jax.experimental import pallas as pl
from jax.experimental.pallas import tpu as pltpu

F32 = jnp.float32
BF16 = jnp.bfloat16

EPS = 1e-6
NEG = -1e30
SSM_GROUP = 16
HEAD_DIM = 128
MOBA_BLOCK = 256
MOBA_TOPK = 3
CONV_W = 3

LANES = 128
SUBLANES = 8
VMEM_LIMIT_BYTES = 56 * 1024 * 1024

SSM_CHUNK = 32
SSM_LANE_GROUPS = LANES // SSM_GROUP
SSM_STATE_POS = 8
ROW_TILE_PROJ = 1024
COL_TILE_PROJ = 1024
ROW_TILE_GATE = 512
ROW_TILE_MERGE = 512
ROW_TILE_FFN = 512
COL_TILE_FFN = 512
FFN_STRIP = 256
K_TILE_GATE = 512
NORM_ROWS = 128
ATTN_Q_BLOCKS = 2
ATTN_HEADS = 2

_NT = (((1,), (1,)), ((), ()))
_HI = lax.Precision.HIGHEST


def _params(n_axes):
    return pltpu.CompilerParams(
        dimension_semantics=("arbitrary",) * n_axes,
        vmem_limit_bytes=VMEM_LIMIT_BYTES,
    )


def _rmsnorm_rows(x, g):
    ms = jnp.mean(x * x, axis=-1, keepdims=True)
    return x * lax.rsqrt(ms + EPS) * g


def _rmsnorm_into(h_ref, x_ref, g_ref):
    step_rows = min(NORM_ROWS, x_ref.shape[0])

    def step(r, carry):
        r0 = pl.multiple_of(r * step_rows, step_rows)
        h_ref[pl.ds(r0, step_rows), :] = _rmsnorm_rows(
            x_ref[pl.ds(r0, step_rows), :], g_ref[...]).astype(h_ref.dtype)
        return carry

    lax.fori_loop(0, x_ref.shape[0] // step_rows, step, 0)


def _inproj_kernel(x_ref, g_ref, w_ref, cs_ref, o_ref, hb_ref, h_ref):
    @pl.when(pl.program_id(1) == 0)
    def _():
        def step(r, carry):
            r0 = pl.multiple_of(r * MOBA_BLOCK, MOBA_BLOCK)
            hn = _rmsnorm_rows(x_ref[pl.ds(r0, MOBA_BLOCK), :], g_ref[...])
            h_ref[pl.ds(r0, MOBA_BLOCK), :] = hn.astype(h_ref.dtype)
            hb_ref[r] = jnp.mean(hn, axis=0, keepdims=True)
            return carry

        lax.fori_loop(0, x_ref.shape[0] // MOBA_BLOCK, step, 0)

    acc = jnp.dot(h_ref[...], w_ref[...], preferred_element_type=F32)
    o_ref[...] = (acc * cs_ref[...]).astype(o_ref.dtype)


def _inproj(x2, g, w_bf, layer, col_scale):
    n_rows, d = x2.shape
    d_in = w_bf.shape[2]
    tm = min(ROW_TILE_PROJ, n_rows)
    tn = COL_TILE_PROJ
    assert tm % MOBA_BLOCK == 0
    blocks_per_tile = tm // MOBA_BLOCK
    return pl.pallas_call(
        _inproj_kernel,
        grid=(n_rows // tm, d_in // tn),
        in_specs=[
            pl.BlockSpec((tm, d), lambda i, j: (i, 0)),
            pl.BlockSpec((1, d), lambda i, j: (0, 0)),
            pl.BlockSpec((None, d, tn), lambda i, j: (layer, 0, j)),
            pl.BlockSpec((1, tn), lambda i, j: (0, j)),
        ],
        out_specs=[
            pl.BlockSpec((tm, tn), lambda i, j: (i, j)),
            pl.BlockSpec((blocks_per_tile, 1, d), lambda i, j: (i, 0, 0)),
        ],
        out_shape=[
            jax.ShapeDtypeStruct((n_rows, d_in), BF16),
            jax.ShapeDtypeStruct((n_rows // MOBA_BLOCK, 1, d), F32),
        ],
        scratch_shapes=[pltpu.VMEM((tm, d), BF16)],
        compiler_params=_params(2),
        name="inproj",
    )(x2, g, w_bf, col_scale)


def _kmean_kernel(hb_ref, wk_ref, o_ref, *, nb):
    @pl.when(pl.program_id(0) == 0)
    def _():
        o_ref[...] = jnp.zeros_like(o_ref)

    o_ref[...] += jnp.dot(hb_ref[...], wk_ref[...], preferred_element_type=F32, precision=_HI)

    @pl.when(pl.program_id(0) == pl.num_programs(0) - 1)
    def _():
        r = lax.broadcasted_iota(jnp.int32, o_ref.shape, 0)
        c = lax.broadcasted_iota(jnp.int32, o_ref.shape, 1)
        n_heads = o_ref.shape[1] // HEAD_DIM
        keep = (c // HEAD_DIM) == (r % n_heads)
        o_ref[...] = jnp.where(keep, o_ref[...], 0.0)


def _kmean(hbar_t, w_in, layer, nb, d_ssm, d_attn):
    rows, d = hbar_t.shape
    k_col_block = (d_ssm + d_attn) // d_attn
    tk = K_TILE_GATE
    return pl.pallas_call(
        functools.partial(_kmean_kernel, nb=nb),
        grid=(d // tk,),
        in_specs=[
            pl.BlockSpec((rows, tk), lambda kk: (0, kk)),
            pl.BlockSpec((None, tk, d_attn), lambda kk: (layer, kk, k_col_block)),
        ],
        out_specs=pl.BlockSpec((rows, d_attn), lambda kk: (0, 0)),
        out_shape=jax.ShapeDtypeStruct((rows, d_attn), F32),
        compiler_params=_params(1),
        name="kmean",
    )(hbar_t, w_in)


def _gatevec_kernel(wq_ref, kbd_ref, o_ref):
    o_ref[0] = lax.dot_general(wq_ref[...], kbd_ref[0], _NT, preferred_element_type=F32,
                               precision=_HI)


def _gatevec(w_in, layer, kbd, d_ssm):
    bsz, n_gate, d_attn = kbd.shape
    d = w_in.shape[1]
    q_col_block = d_ssm // d_attn
    tk = K_TILE_GATE
    return pl.pallas_call(
        _gatevec_kernel,
        grid=(bsz, d // tk),
        in_specs=[
            pl.BlockSpec((None, tk, d_attn), lambda b, kk: (layer, kk, q_col_block)),
            pl.BlockSpec((1, n_gate, d_attn), lambda b, kk: (b, 0, 0)),
        ],
        out_specs=pl.BlockSpec((1, tk, n_gate), lambda b, kk: (b, kk, 0)),
        out_shape=jax.ShapeDtypeStruct((bsz, d, n_gate), F32),
        compiler_params=_params(2),
        name="gatevec",
    )(w_in, kbd)


def _select_kernel(x_ref, g_ref, z_ref, o_ref, *, nb, tq):
    h = _rmsnorm_rows(x_ref[0], g_ref[...])
    gate = jnp.dot(h, z_ref[0], preferred_element_type=F32, precision=_HI)
    n_heads = gate.shape[1] // nb
    row = lax.broadcasted_iota(jnp.int32, gate.shape, 0) + pl.program_id(1) * tq
    qblk = row // MOBA_BLOCK
    j = lax.broadcasted_iota(jnp.int32, gate.shape, 1) // n_heads
    past = j < qblk
    gate = jnp.where(past, gate, NEG)
    cnt = jnp.zeros(gate.shape, F32)
    for r in range(1, nb):
        other = pltpu.roll(gate, r * n_heads, 1)
        wins_tie = jnp.where(other >= gate, 1.0, 0.0)
        wins_strict = jnp.where(other > gate, 1.0, 0.0)
        cnt = cnt + jnp.where(j >= r, wins_tie, wins_strict)
    keep = jnp.logical_or(jnp.logical_and(past, cnt < float(MOBA_TOPK)), j == qblk)
    o_ref[0] = jnp.where(keep, 0.0, NEG)


def _select(x3, g, z, nb):
    bsz, s, d = x3.shape
    n_gate = z.shape[2]
    tq = min(ROW_TILE_GATE, s)
    return pl.pallas_call(
        functools.partial(_select_kernel, nb=nb, tq=tq),
        grid=(bsz, s // tq),
        in_specs=[
            pl.BlockSpec((1, tq, d), lambda b, i: (b, i, 0)),
            pl.BlockSpec((1, d), lambda b, i: (0, 0)),
            pl.BlockSpec((1, d, n_gate), lambda b, i: (b, 0, 0)),
        ],
        out_specs=pl.BlockSpec((1, tq, n_gate), lambda b, i: (b, i, 0)),
        out_shape=jax.ShapeDtypeStruct((bsz, s, n_gate), F32),
        compiler_params=_params(2),
        name="select",
    )(x3, g, z)


def _attn_kernel(q_ref, k_ref, v_ref, bias_ref, o_ref, qa_ref, s_ref, m_ref, acc_ref, *, nb):
    hp = pl.program_id(1)
    i = pl.program_id(2)
    blk = MOBA_BLOCK
    tq = q_ref.shape[1]
    n_gate = bias_ref.shape[2]
    heads = range(ATTN_HEADS)
    mask_rows = bias_ref[0].astype(BF16)
    for hh in heads:
        qa_ref[hh, :, :HEAD_DIM] = q_ref[0, :, hh * HEAD_DIM:(hh + 1) * HEAD_DIM]
        qa_ref[hh, :, HEAD_DIM:] = mask_rows
    m_ref[...] = jnp.full(m_ref.shape, -jnp.inf, F32)
    acc_ref[...] = jnp.zeros(acc_ref.shape, F32)
    lane = lax.broadcasted_iota(jnp.int32, (blk, n_gate), 1)
    ones = jnp.ones((blk, HEAD_DIM), BF16)

    def scores(slot, hh, jb):
        st = pl.multiple_of(jb * blk, blk)
        col = jb * (n_gate // nb) + hp * ATTN_HEADS + hh
        onehot = jnp.where(lane == col, 1.0, 0.0).astype(BF16)
        kj = k_ref[0, pl.ds(st, blk), hh * HEAD_DIM:(hh + 1) * HEAD_DIM]
        s_ref[slot, hh] = lax.dot_general(qa_ref[hh], jnp.concatenate([kj, onehot], axis=1), _NT,
                                          preferred_element_type=F32)

    def update(slot, hh, jb, causal):
        s = s_ref[slot, hh]
        if causal:
            qpos = lax.broadcasted_iota(jnp.int32, s.shape, 0) + i * tq
            kpos = lax.broadcasted_iota(jnp.int32, s.shape, 1) + jb * blk
            s = jnp.where(kpos <= qpos, s, NEG)
        st = pl.multiple_of(jb * blk, blk)
        m_old = m_ref[hh]
        m_new = jnp.maximum(m_old, jnp.max(s, axis=-1, keepdims=True))
        alpha = jnp.exp(m_old - m_new)
        p = jnp.exp(s - jnp.concatenate([m_new] * (blk // LANES), axis=1))
        vj = v_ref[0, pl.ds(st, blk), hh * HEAD_DIM:(hh + 1) * HEAD_DIM]
        acc_ref[hh] = jnp.concatenate([alpha, alpha], axis=1) * acc_ref[hh] + jnp.dot(
            p.astype(BF16), jnp.concatenate([vj, ones], axis=1), preferred_element_type=F32)
        m_ref[hh] = m_new

    for hh in heads:
        scores(0, hh, 0)

    def pair(jj, carry):
        for hh in heads:
            scores(1, hh, 2 * jj + 1)
        for hh in heads:
            update(0, hh, 2 * jj, False)
        for hh in heads:
            scores(0, hh, 2 * jj + 2)
        for hh in heads:
            update(1, hh, 2 * jj + 1, False)
        return carry

    lax.fori_loop(0, i, pair, 0)
    for hh in heads:
        scores(1, hh, 2 * i + 1)
    for hh in heads:
        update(0, hh, 2 * i, True)
    for hh in heads:
        update(1, hh, 2 * i + 1, True)
    for hh in heads:
        o_ref[0, :, hh * HEAD_DIM:(hh + 1) * HEAD_DIM] = (
            acc_ref[hh, :, :HEAD_DIM] / acc_ref[hh, :, HEAD_DIM:]).astype(o_ref.dtype)


def _attention(proj3, bias, nb, d_ssm):
    bsz, s, _ = proj3.shape
    n_gate = bias.shape[2]
    n_heads = n_gate // nb
    d_attn = n_heads * HEAD_DIM
    q0 = d_ssm // HEAD_DIM
    k0 = q0 + n_heads
    v0 = k0 + n_heads
    assert ATTN_Q_BLOCKS == 2 and nb % ATTN_Q_BLOCKS == 0
    assert n_heads % ATTN_HEADS == 0 and q0 % ATTN_HEADS == 0
    tq = ATTN_Q_BLOCKS * MOBA_BLOCK
    hw = ATTN_HEADS * HEAD_DIM
    return pl.pallas_call(
        functools.partial(_attn_kernel, nb=nb),
        grid=(bsz, n_heads // ATTN_HEADS, s // tq),
        in_specs=[
            pl.BlockSpec((1, tq, hw), lambda b, h, i: (b, i, q0 // ATTN_HEADS + h)),
            pl.BlockSpec((1, s, hw), lambda b, h, i: (b, 0, k0 // ATTN_HEADS + h)),
            pl.BlockSpec((1, s, hw), lambda b, h, i: (b, 0, v0 // ATTN_HEADS + h)),
            pl.BlockSpec((1, tq, n_gate), lambda b, h, i: (b, i, 0)),
        ],
        out_specs=pl.BlockSpec((1, tq, hw), lambda b, h, i: (b, i, h)),
        out_shape=jax.ShapeDtypeStruct((bsz, s, d_attn), BF16),
        scratch_shapes=[
            pltpu.VMEM((ATTN_HEADS, tq, HEAD_DIM + n_gate), BF16),
            pltpu.VMEM((2, ATTN_HEADS, tq, MOBA_BLOCK), F32),
            pltpu.VMEM((ATTN_HEADS, tq, LANES), F32),
            pltpu.VMEM((ATTN_HEADS, tq, 2 * HEAD_DIM), F32),
        ],
        compiler_params=_params(3),
        name="moba_attn",
    )(proj3, proj3, proj3, bias)


def _ssm_tables(lam_re, lam_im, log_dt, b_re, b_im, c_re, c_im, d_skip):
    t_len = SSM_CHUNK
    n_groups, n_state = lam_re.shape
    n_ch = SSM_GROUP
    assert 2 * n_state == LANES
    w = t_len * n_ch
    dt = jnp.exp(log_dt)[:, None]
    ar = lam_re * dt
    ai = lam_im * dt
    steps = jnp.arange(t_len + 1, dtype=F32)[None, :, None]
    mag = jnp.exp(ar[:, None, :] * steps)
    pw_re = mag * jnp.cos(ai[:, None, :] * steps)
    pw_im = mag * jnp.sin(ai[:, None, :] * steps)
    e1 = jnp.expm1(ar)
    sh = jnp.sin(0.5 * ai)
    num_re = e1 * jnp.cos(ai) - 2.0 * sh * sh
    num_im = (e1 + 1.0) * jnp.sin(ai)
    den = lam_re * lam_re + lam_im * lam_im
    coef_re = (num_re * lam_re + num_im * lam_im) / den
    coef_im = (num_im * lam_re - num_re * lam_im) / den
    bb_re = coef_re[..., None] * b_re - coef_im[..., None] * b_im
    bb_im = coef_re[..., None] * b_im + coef_im[..., None] * b_re
    cp_re = c_re[:, None] * pw_re[:, :, None, :] - c_im[:, None] * pw_im[:, :, None, :]
    cp_im = c_re[:, None] * pw_im[:, :, None, :] + c_im[:, None] * pw_re[:, :, None, :]
    cpow = jnp.concatenate([cp_re, -cp_im], axis=-1).reshape(n_groups, (t_len + 1) * n_ch, LANES)
    lane_of = (jnp.arange(n_groups) % SSM_LANE_GROUPS)[:, None] * n_ch + jnp.arange(n_ch)[None, :]
    place = (lane_of[:, :, None] == jnp.arange(LANES)[None, None, :]).astype(F32)
    bshift = jnp.einsum('gpm,gml->gpl', jnp.concatenate([bb_re, bb_im], axis=1), place, precision=_HI)
    rev_re = pw_re[:, :t_len][:, ::-1][:, :, None, :]
    rev_im = pw_im[:, :t_len][:, ::-1][:, :, None, :]
    bt_re = bb_re.transpose(0, 2, 1)[:, None]
    bt_im = bb_im.transpose(0, 2, 1)[:, None]
    pin = jnp.stack([rev_re * bt_re - rev_im * bt_im, rev_re * bt_im + rev_im * bt_re], axis=1)
    pad_p = LANES - n_state
    pin = jnp.pad(pin.reshape(n_groups, 2, w, n_state), ((0, 0), (0, 0), (0, 0), (0, pad_p)))
    adec = jnp.stack([pw_re[:, t_len], pw_im[:, t_len]], axis=1)[:, :, None, :]
    adec = jnp.pad(adec, ((0, 0), (0, 0), (0, 0), (0, pad_p)))
    return cpow, bshift, pin.astype(BF16), adec


def _ssm_kernel(x_ref, cpow_ref, bshift_ref, pin_ref, adec_ref, dvec_ref, o_ref,
                rs_ref, xall_ref, hre_ref, him_ref, kk_ref, wt_ref, wrev_ref, pbd_ref, qbd_ref,
                *, bsz, nc):
    t_len = SSM_CHUNK
    n_pairs = t_len // 2
    ng = SSM_LANE_GROUPS
    gw = SSM_GROUP
    n_state = LANES // 2
    n_chunks = bsz * nc
    wide = ng * LANES

    for g in range(ng):
        kk_ref[g] = jnp.dot(cpow_ref[g, 0:t_len * gw, :], bshift_ref[g], preferred_element_type=F32,
                            precision=_HI)
    row = lax.broadcasted_iota(jnp.int32, (LANES, LANES), 0)
    col = lax.broadcasted_iota(jnp.int32, (LANES, LANES), 1)
    wt_ref[0] = jnp.zeros((LANES, LANES), BF16)
    for l in range(t_len):
        tile = kk_ref[:, l * gw:(l + 1) * gw, :].reshape(LANES, LANES)
        if l == 0:
            tile = tile + jnp.where(row == col, dvec_ref[0], 0.0)
        wt_ref[l + 1] = tile.astype(BF16)
    pw = 2 * LANES
    for d in range(n_pairs):
        c0 = (n_pairs - 1 - d) * pw
        wrev_ref[0:LANES, c0:c0 + LANES] = wt_ref[2 * d + 1]
        wrev_ref[0:LANES, c0 + LANES:c0 + pw] = wt_ref[2 * d]
        wrev_ref[LANES:, c0:c0 + LANES] = wt_ref[2 * d + 2]
        wrev_ref[LANES:, c0 + LANES:c0 + pw] = wt_ref[2 * d + 1]

    rs_ref[...] = x_ref[...].astype(F32)
    for s in range(t_len):
        xall_ref[:, s * LANES:(s + 1) * LANES] = rs_ref[pl.ds(s, n_chunks, stride=t_len), :].astype(BF16)

    pbd_ref[...] = jnp.zeros(pbd_ref.shape, BF16)
    for k in range(t_len // SSM_STATE_POS):
        for sl in range(SSM_STATE_POS):
            s0 = (k * SSM_STATE_POS + sl) * gw
            for g in range(ng):
                rows = slice(sl * LANES + g * gw, sl * LANES + (g + 1) * gw)
                lanes = slice(g * LANES, (g + 1) * LANES)
                pbd_ref[0, rows, lanes] = pin_ref[g, 0, s0:s0 + gw, :]
                pbd_ref[1, rows, lanes] = pin_ref[g, 1, s0:s0 + gw, :]
        xk = xall_ref[:, k * SSM_STATE_POS * LANES:(k + 1) * SSM_STATE_POS * LANES]
        h_re = jnp.dot(xk, pbd_ref[0], preferred_element_type=F32)
        h_im = jnp.dot(xk, pbd_ref[1], preferred_element_type=F32)
        if k == 0:
            hre_ref[...] = h_re
            him_ref[...] = h_im
        else:
            hre_ref[...] += h_re
            him_ref[...] += h_im

    a_re = jnp.concatenate([adec_ref[g, 0] for g in range(ng)], axis=1)
    a_im = jnp.concatenate([adec_ref[g, 1] for g in range(ng)], axis=1)

    def step(c, carry):
        new = []
        for b in range(bsz):
            s_re, s_im = carry[b]
            r = b * nc + c
            loc_re = hre_ref[pl.ds(r, 1), :]
            loc_im = him_ref[pl.ds(r, 1), :]
            hre_ref[pl.ds(r, 1), :] = s_re
            him_ref[pl.ds(r, 1), :] = s_im
            new.append((a_re * s_re - a_im * s_im + loc_re,
                        a_re * s_im + a_im * s_re + loc_im))
        return tuple(new)

    zero = jnp.zeros((1, wide), F32)
    lax.fori_loop(0, nc, step, tuple((zero, zero) for _ in range(bsz)), unroll=2)

    h_in = (hre_ref[...] + pltpu.roll(him_ref[...], n_state, 1)).astype(BF16)
    qbd_ref[...] = jnp.zeros(qbd_ref.shape, BF16)
    for q in range(n_pairs):
        for tl in range(2):
            s0 = (2 * q + tl + 1) * gw
            for g in range(ng):
                qbd_ref[tl * LANES + g * gw:tl * LANES + (g + 1) * gw, g * LANES:(g + 1) * LANES] = (
                    cpow_ref[g, s0:s0 + gw, :].astype(BF16))
        y = lax.dot_general(h_in, qbd_ref[...], _NT, preferred_element_type=F32)
        y = y + lax.dot_general(xall_ref[:, 0:(q + 1) * pw], wrev_ref[:, (n_pairs - 1 - q) * pw:], _NT,
                                preferred_element_type=F32)
        y = jax.nn.gelu(y)
        rs_ref[pl.ds(2 * q, n_chunks, stride=t_len), :] = y[:, 0:LANES]
        rs_ref[pl.ds(2 * q + 1, n_chunks, stride=t_len), :] = y[:, LANES:]
    o_ref[...] = rs_ref[...].astype(o_ref.dtype)


def _ssm(proj2, tables, dvec, layer, bsz, d_ssm):
    cpow, bshift, pin, adec = tables
    rows = proj2.shape[0]
    nc = rows // bsz // SSM_CHUNK
    n_chunks = bsz * nc
    ng = SSM_LANE_GROUPS
    n_pairs = SSM_CHUNK // 2
    wide = ng * LANES
    cp_rows = cpow.shape[2]
    w = SSM_CHUNK * SSM_GROUP
    return pl.pallas_call(
        functools.partial(_ssm_kernel, bsz=bsz, nc=nc),
        grid=(d_ssm // LANES,),
        in_specs=[
            pl.BlockSpec((rows, LANES), lambda j: (0, j), pipeline_mode=pl.Buffered(1)),
            pl.BlockSpec((None, ng, cp_rows, LANES), lambda j: (layer, j, 0, 0)),
            pl.BlockSpec((None, ng, LANES, LANES), lambda j: (layer, j, 0, 0)),
            pl.BlockSpec((None, ng, 2, w, LANES), lambda j: (layer, j, 0, 0, 0)),
            pl.BlockSpec((None, ng, 2, 1, LANES), lambda j: (layer, j, 0, 0, 0)),
            pl.BlockSpec((None, 1, 1, LANES), lambda j: (layer, j, 0, 0)),
        ],
        out_specs=pl.BlockSpec((rows, LANES), lambda j: (0, j)),
        out_shape=jax.ShapeDtypeStruct((rows, d_ssm), BF16),
        scratch_shapes=[
            pltpu.VMEM((rows, LANES), F32),
            pltpu.VMEM((n_chunks, SSM_CHUNK * LANES), BF16),
            pltpu.VMEM((n_chunks, wide), F32),
            pltpu.VMEM((n_chunks, wide), F32),
            pltpu.VMEM((ng, w, LANES), F32),
            pltpu.VMEM((SSM_CHUNK + 1, LANES, LANES), BF16),
            pltpu.VMEM((2 * LANES, n_pairs * 2 * LANES), BF16),
            pltpu.VMEM((2, SSM_STATE_POS * LANES, wide), BF16),
            pltpu.VMEM((2 * LANES, wide), BF16),
        ],
        compiler_params=_params(1),
        name="s5_chunked",
    )(proj2, cpow, bshift, pin, adec, dvec)


def _merge_kernel(y_ref, att_ref, ga_ref, gb_ref, x_ref, wglu_ref, wus_ref, wua_ref, wout_ref,
                  g_ref, o_ref):
    y = y_ref[...]
    z = jnp.dot(y, wglu_ref[...], preferred_element_type=F32)
    s5 = (y.astype(F32) * jax.nn.sigmoid(z)).astype(BF16)
    ys = jnp.dot(s5, wus_ref[...], preferred_element_type=F32)
    ya = jnp.dot(att_ref[...], wua_ref[...], preferred_element_type=F32)
    m = (jax.nn.sigmoid(ga_ref[...].astype(F32)) * ys
         + jax.nn.sigmoid(gb_ref[...].astype(F32)) * ya)
    o = jnp.dot(m.astype(BF16), wout_ref[...], preferred_element_type=F32)
    o_ref[...] = x_ref[...] + _rmsnorm_rows(o, g_ref[...])


def _merge(y2, att2, proj2, x2, w_glu, w_us, w_ua, w_out, layer, g_post):
    n_rows, d = x2.shape
    d_ssm = y2.shape[1]
    d_attn = att2.shape[1]
    tm = min(ROW_TILE_MERGE, n_rows)
    ga_blk = (d_ssm + 3 * d_attn) // d

    def weight(wt):
        return pl.BlockSpec((None,) + wt.shape[1:], lambda i: (layer, 0, 0),
                            pipeline_mode=pl.Buffered(1))

    return pl.pallas_call(
        _merge_kernel,
        grid=(n_rows // tm,),
        in_specs=[
            pl.BlockSpec((tm, d_ssm), lambda i: (i, 0)),
            pl.BlockSpec((tm, d_attn), lambda i: (i, 0)),
            pl.BlockSpec((tm, d), lambda i: (i, ga_blk)),
            pl.BlockSpec((tm, d), lambda i: (i, ga_blk + 1)),
            pl.BlockSpec((tm, d), lambda i: (i, 0)),
            weight(w_glu), weight(w_us), weight(w_ua), weight(w_out),
            pl.BlockSpec((1, d), lambda i: (0, 0)),
        ],
        out_specs=pl.BlockSpec((tm, d), lambda i: (i, 0)),
        out_shape=jax.ShapeDtypeStruct((n_rows, d), F32),
        compiler_params=_params(1),
        name="merge",
    )(y2, att2, proj2, proj2, x2, w_glu, w_us, w_ua, w_out, g_post)


def _ffn_kernel(x_ref, gpre_ref, wa_ref, wv_ref, cwa_ref, cwv_ref, cba_ref, cbv_ref, wd_ref,
                gpost_ref, o_ref, h_ref, za_ref, zv_ref, halo_a_ref, halo_v_ref,
                *, tiles_per_seq):
    i = pl.program_id(0)
    c = pl.program_id(1)
    tm = x_ref.shape[0]

    @pl.when(c == 0)
    def _():
        _rmsnorm_into(h_ref, x_ref, gpre_ref)
        o_ref[...] = jnp.zeros_like(o_ref)

    @pl.when((i % tiles_per_seq) == 0)
    def _():
        halo_a_ref[c] = jnp.zeros(halo_a_ref.shape[1:], F32)
        halo_v_ref[c] = jnp.zeros(halo_v_ref.shape[1:], F32)

    za_ref[0:SUBLANES, :] = halo_a_ref[c]
    zv_ref[0:SUBLANES, :] = halo_v_ref[c]
    cwa, cwv = cwa_ref[...], cwv_ref[...]
    cba, cbv = cba_ref[...], cbv_ref[...]

    def up(r0):
        h = h_ref[r0:r0 + FFN_STRIP, :]
        out_rows = slice(SUBLANES + r0, SUBLANES + r0 + FFN_STRIP)
        za_ref[out_rows, :] = jnp.dot(h, wa_ref[...], preferred_element_type=F32)
        zv_ref[out_rows, :] = jnp.dot(h, wv_ref[...], preferred_element_type=F32)

    def conv_strip(z_ref, cw, bias, r0):
        base = SUBLANES + r0
        z0 = z_ref[base:base + FFN_STRIP, :]
        z1 = z_ref[base - 1:base - 1 + FFN_STRIP, :]
        z2 = z_ref[base - 2:base - 2 + FFN_STRIP, :]
        return cw[0:1, :] * z2 + cw[1:2, :] * z1 + cw[2:3, :] * z0 + bias

    def finish(r0):
        a = conv_strip(za_ref, cwa, cba, r0)
        v = conv_strip(zv_ref, cwv, cbv, r0)
        gated = (a * jax.nn.sigmoid(a) * v).astype(BF16)
        o_ref[r0:r0 + FFN_STRIP, :] += jnp.dot(gated, wd_ref[...], preferred_element_type=F32)

    strips = list(range(0, tm, FFN_STRIP))
    up(strips[0])
    for k, r0 in enumerate(strips):
        if k + 1 < len(strips):
            up(strips[k + 1])
        finish(r0)

    halo_a_ref[c] = za_ref[tm:tm + SUBLANES, :]
    halo_v_ref[c] = zv_ref[tm:tm + SUBLANES, :]

    @pl.when(c == pl.num_programs(1) - 1)
    def _():
        step_rows = min(NORM_ROWS, tm)

        def finalize(r, carry):
            rows = pl.ds(pl.multiple_of(r * step_rows, step_rows), step_rows)
            o_ref[rows, :] = x_ref[rows, :] + _rmsnorm_rows(o_ref[rows, :], gpost_ref[...])
            return carry

        lax.fori_loop(0, tm // step_rows, finalize, 0)


def _ffn(x2, g_pre, w_up, conv_w, conv_b, w_down, layer, g_post, seq_len):
    n_rows, d = x2.shape
    d_ff = w_down.shape[1]
    tm = min(ROW_TILE_FFN, seq_len)
    tf = COL_TILE_FFN
    nff = d_ff // tf
    assert tm % FFN_STRIP == 0
    const = lambda i, c: (0, 0)
    zbuf = pltpu.VMEM((SUBLANES + tm, tf), F32)
    return pl.pallas_call(
        functools.partial(_ffn_kernel, tiles_per_seq=seq_len // tm),
        grid=(n_rows // tm, nff),
        in_specs=[
            pl.BlockSpec((tm, d), lambda i, c: (i, 0)),
            pl.BlockSpec((1, d), const),
            pl.BlockSpec((None, d, tf), lambda i, c: (layer, 0, c)),
            pl.BlockSpec((None, d, tf), lambda i, c: (layer, 0, nff + c)),
            pl.BlockSpec((None, CONV_W, tf), lambda i, c: (layer, 0, c)),
            pl.BlockSpec((None, CONV_W, tf), lambda i, c: (layer, 0, nff + c)),
            pl.BlockSpec((None, 1, tf), lambda i, c: (layer, 0, c)),
            pl.BlockSpec((None, 1, tf), lambda i, c: (layer, 0, nff + c)),
            pl.BlockSpec((None, tf, d), lambda i, c: (layer, c, 0)),
            pl.BlockSpec((1, d), const),
        ],
        out_specs=pl.BlockSpec((tm, d), lambda i, c: (i, 0)),
        out_shape=jax.ShapeDtypeStruct((n_rows, d), F32),
        scratch_shapes=[
            pltpu.VMEM((tm, d), BF16),
            zbuf, zbuf,
            pltpu.VMEM((nff, SUBLANES, tf), F32),
            pltpu.VMEM((nff, SUBLANES, tf), F32),
        ],
        compiler_params=_params(2),
        name="convglu_ffn",
    )(x2, g_pre, w_up, w_up, conv_w, conv_w, conv_b, conv_b, w_down, g_post)


def kernel(x, g_pre_mix, w_in, lam_re, lam_im, log_dt, b_re, b_im, c_re, c_im, d_skip, w_glu,
           w_up_ssm, w_up_attn, w_out, g_post_mix, g_pre_ffn, w_ffn_up, conv_w, conv_b,
           w_ffn_down, g_post_ffn):
    bsz, seq_len, d = x.shape
    depth = w_in.shape[0]
    d_ssm = w_glu.shape[1]
    d_attn = w_up_attn.shape[1]
    n_heads = d_attn // HEAD_DIM
    n_groups = d_ssm // SSM_GROUP
    assert seq_len % (ATTN_Q_BLOCKS * MOBA_BLOCK) == 0 and seq_len % SSM_CHUNK == 0
    assert d_ssm % d_attn == 0
    nb = seq_len // MOBA_BLOCK
    nc = seq_len // SSM_CHUNK
    n_rows = bsz * seq_len
    row = lambda v: v.reshape(1, -1)

    w_in_bf = w_in.astype(BF16)
    w_glu_bf = w_glu.astype(BF16)
    w_us_bf = w_up_ssm.astype(BF16)
    w_ua_bf = w_up_attn.astype(BF16)
    w_out_bf = w_out.astype(BF16)
    w_fu_bf = w_ffn_up.astype(BF16)
    w_fd_bf = w_ffn_down.astype(BF16)
    conv_b3 = conv_b[:, None, :]
    tables = jax.vmap(_ssm_tables)(lam_re, lam_im, log_dt, b_re, b_im, c_re, c_im, d_skip)
    dvec = d_skip.reshape(depth, d_ssm // LANES, 1, LANES)
    col_scale = jnp.ones((w_in.shape[2],), F32).at[d_ssm:d_ssm + d_attn].set(1.0 / math.sqrt(HEAD_DIM))

    x2 = x.reshape(n_rows, d)
    for l in range(depth):
        proj2, hbar = _inproj(x2, row(g_pre_mix[l]), w_in_bf, l, row(col_scale))

        hbar = hbar.reshape(bsz, nb, 1, d)
        hbar_t = jnp.broadcast_to(hbar, (bsz, nb, n_heads, d)).reshape(bsz * nb * n_heads, d)
        kbd = _kmean(hbar_t, w_in, l, nb, d_ssm, d_attn).reshape(bsz, n_heads * nb, d_attn)
        z = _gatevec(w_in, l, kbd, d_ssm)
        bias = _select(x2.reshape(bsz, seq_len, d), row(g_pre_mix[l]), z, nb)
        att = _attention(proj2.reshape(bsz, seq_len, -1), bias, nb, d_ssm)

        y2 = _ssm(proj2, tables, dvec, l, bsz, d_ssm)

        x2 = _merge(y2, att.reshape(n_rows, d_attn), proj2, x2, w_glu_bf, w_us_bf, w_ua_bf,
                    w_out_bf, l, row(g_post_mix[l]))
        x2 = _ffn(x2, row(g_pre_ffn[l]), w_fu_bf, conv_w, conv_b3, w_fd_bf, l,
                  row(g_post_ffn[l]), seq_len)
    return x2.reshape(bsz, seq_len, d)
```

```python
import functools
import math

import jax
import jax.numpy as jnp
from jax import lax
from jax.experimental import pallas as pl
from jax.experimental.pallas import tpu as pltpu

F32 = jnp.float32
BF16 = jnp.bfloat16

EPS = 1e-6
NEG = -1e30
SSM_GROUP = 16
HEAD_DIM = 128
MOBA_BLOCK = 256
MOBA_TOPK = 3
CONV_W = 3

LANES = 128
SUBLANES = 8
VMEM_LIMIT_BYTES = 56 * 1024 * 1024

SSM_CHUNK = 32
SSM_LANE_GROUPS = LANES // SSM_GROUP
SSM_STATE_POS = 8
ROW_TILE_PROJ = 1024
COL_TILE_PROJ = 2048
ROW_TILE_GATE = 512
ROW_TILE_MERGE = 512
MERGE_STRIP = 256
ROW_TILE_FFN = 512
COL_TILE_FFN = 512
FFN_STRIP = 256
K_TILE_GATE = 512
NORM_ROWS = 128
ATTN_Q_BLOCKS = 2
ATTN_HEADS = 2

_NT = (((1,), (1,)), ((), ()))
_HI = lax.Precision.HIGHEST


def _params(n_axes):
    return pltpu.CompilerParams(
        dimension_semantics=("arbitrary",) * n_axes,
        vmem_limit_bytes=VMEM_LIMIT_BYTES,
    )


def _rmsnorm_rows(x, g):
    ms = jnp.mean(x * x, axis=-1, keepdims=True)
    return x * lax.rsqrt(ms + EPS) * g


def _rmsnorm_into(h_ref, x_ref, g_ref):
    step_rows = min(NORM_ROWS, x_ref.shape[0])

    def step(r, carry):
        r0 = pl.multiple_of(r * step_rows, step_rows)
        h_ref[pl.ds(r0, step_rows), :] = _rmsnorm_rows(
            x_ref[pl.ds(r0, step_rows), :], g_ref[...]).astype(h_ref.dtype)
        return carry

    lax.fori_loop(0, x_ref.shape[0] // step_rows, step, 0)


def _inproj_kernel(x_ref, g_ref, w_ref, cs_ref, o_ref, hb_ref, h_ref):
    @pl.when(pl.program_id(1) == 0)
    def _():
        def step(r, carry):
            r0 = pl.multiple_of(r * MOBA_BLOCK, MOBA_BLOCK)
            hn = _rmsnorm_rows(x_ref[pl.ds(r0, MOBA_BLOCK), :], g_ref[...])
            h_ref[pl.ds(r0, MOBA_BLOCK), :] = hn.astype(h_ref.dtype)
            hb_ref[r] = jnp.mean(hn, axis=0, keepdims=True)
            return carry

        lax.fori_loop(0, x_ref.shape[0] // MOBA_BLOCK, step, 0)

    acc = jnp.dot(h_ref[...], w_ref[...], preferred_element_type=F32)
    o_ref[...] = (acc * cs_ref[...]).astype(o_ref.dtype)


def _inproj(x2, g, w_bf, layer, col_scale):
    n_rows, d = x2.shape
    d_in = w_bf.shape[2]
    tm = min(ROW_TILE_PROJ, n_rows)
    tn = COL_TILE_PROJ
    assert tm % MOBA_BLOCK == 0
    blocks_per_tile = tm // MOBA_BLOCK
    return pl.pallas_call(
        _inproj_kernel,
        grid=(n_rows // tm, d_in // tn),
        in_specs=[
            pl.BlockSpec((tm, d), lambda i, j: (i, 0)),
            pl.BlockSpec((1, d), lambda i, j: (0, 0)),
            pl.BlockSpec((None, d, tn), lambda i, j: (layer, 0, j)),
            pl.BlockSpec((1, tn), lambda i, j: (0, j)),
        ],
        out_specs=[
            pl.BlockSpec((tm, tn), lambda i, j: (i, j)),
            pl.BlockSpec((blocks_per_tile, 1, d), lambda i, j: (i, 0, 0)),
        ],
        out_shape=[
            jax.ShapeDtypeStruct((n_rows, d_in), BF16),
            jax.ShapeDtypeStruct((n_rows // MOBA_BLOCK, 1, d), F32),
        ],
        scratch_shapes=[pltpu.VMEM((tm, d), BF16)],
        compiler_params=_params(2),
        name="inproj",
    )(x2, g, w_bf, col_scale)


def _kmean_kernel(hb_ref, wk_ref, o_ref, *, nb):
    @pl.when(pl.program_id(0) == 0)
    def _():
        o_ref[...] = jnp.zeros_like(o_ref)

    o_ref[...] += jnp.dot(hb_ref[...], wk_ref[...], preferred_element_type=F32, precision=_HI)

    @pl.when(pl.program_id(0) == pl.num_programs(0) - 1)
    def _():
        r = lax.broadcasted_iota(jnp.int32, o_ref.shape, 0)
        c = lax.broadcasted_iota(jnp.int32, o_ref.shape, 1)
        n_heads = o_ref.shape[1] // HEAD_DIM
        keep = (c // HEAD_DIM) == (r % n_heads)
        o_ref[...] = jnp.where(keep, o_ref[...], 0.0)


def _kmean(hbar_t, w_in, layer, nb, d_ssm, d_attn):
    rows, d = hbar_t.shape
    k_col_block = (d_ssm + d_attn) // d_attn
    tk = K_TILE_GATE
    return pl.pallas_call(
        functools.partial(_kmean_kernel, nb=nb),
        grid=(d // tk,),
        in_specs=[
            pl.BlockSpec((rows, tk), lambda kk: (0, kk)),
            pl.BlockSpec((None, tk, d_attn), lambda kk: (layer, kk, k_col_block)),
        ],
        out_specs=pl.BlockSpec((rows, d_attn), lambda kk: (0, 0)),
        out_shape=jax.ShapeDtypeStruct((rows, d_attn), F32),
        compiler_params=_params(1),
        name="kmean",
    )(hbar_t, w_in)


def _gatevec_kernel(wq_ref, kbd_ref, o_ref):
    o_ref[0] = lax.dot_general(wq_ref[...], kbd_ref[0], _NT, preferred_element_type=F32,
                               precision=_HI)


def _gatevec(w_in, layer, kbd, d_ssm):
    bsz, n_gate, d_attn = kbd.shape
    d = w_in.shape[1]
    q_col_block = d_ssm // d_attn
    tk = K_TILE_GATE
    return pl.pallas_call(
        _gatevec_kernel,
        grid=(bsz, d // tk),
        in_specs=[
            pl.BlockSpec((None, tk, d_attn), lambda b, kk: (layer, kk, q_col_block)),
            pl.BlockSpec((1, n_gate, d_attn), lambda b, kk: (b, 0, 0)),
        ],
        out_specs=pl.BlockSpec((1, tk, n_gate), lambda b, kk: (b, kk, 0)),
        out_shape=jax.ShapeDtypeStruct((bsz, d, n_gate), F32),
        compiler_params=_params(2),
        name="gatevec",
    )(w_in, kbd)


def _select_kernel(x_ref, g_ref, z_ref, o_ref, *, nb, tq):
    h = _rmsnorm_rows(x_ref[0], g_ref[...])
    gate = jnp.dot(h, z_ref[0], preferred_element_type=F32, precision=_HI)
    n_heads = gate.shape[1] // nb
    row = lax.broadcasted_iota(jnp.int32, gate.shape, 0) + pl.program_id(1) * tq
    qblk = row // MOBA_BLOCK
    j = lax.broadcasted_iota(jnp.int32, gate.shape, 1) // n_heads
    past = j < qblk
    gate = jnp.where(past, gate, NEG)
    cnt = jnp.zeros(gate.shape, F32)
    for r in range(1, nb):
        other = pltpu.roll(gate, r * n_heads, 1)
        wins_tie = jnp.where(other >= gate, 1.0, 0.0)
        wins_strict = jnp.where(other > gate, 1.0, 0.0)
        cnt = cnt + jnp.where(j >= r, wins_tie, wins_strict)
    keep = jnp.logical_or(jnp.logical_and(past, cnt < float(MOBA_TOPK)), j == qblk)
    o_ref[0] = jnp.where(keep, 0.0, NEG)


def _select(x3, g, z, nb):
    bsz, s, d = x3.shape
    n_gate = z.shape[2]
    tq = min(ROW_TILE_GATE, s)
    return pl.pallas_call(
        functools.partial(_select_kernel, nb=nb, tq=tq),
        grid=(bsz, s // tq),
        in_specs=[
            pl.BlockSpec((1, tq, d), lambda b, i: (b, i, 0)),
            pl.BlockSpec((1, d), lambda b, i: (0, 0)),
            pl.BlockSpec((1, d, n_gate), lambda b, i: (b, 0, 0)),
        ],
        out_specs=pl.BlockSpec((1, tq, n_gate), lambda b, i: (b, i, 0)),
        out_shape=jax.ShapeDtypeStruct((bsz, s, n_gate), F32),
        compiler_params=_params(2),
        name="select",
    )(x3, g, z)


def _attn_kernel(q_ref, k_ref, v_ref, bias_ref, o_ref, qa_ref, s_ref, m_ref, acc_ref, *, nb):
    hp = pl.program_id(1)
    i = pl.program_id(2)
    blk = MOBA_BLOCK
    tq = q_ref.shape[1]
    n_gate = bias_ref.shape[2]
    heads = range(ATTN_HEADS)
    mask_rows = bias_ref[0].astype(BF16)
    for hh in heads:
        qa_ref[hh, :, :HEAD_DIM] = q_ref[0, :, hh * HEAD_DIM:(hh + 1) * HEAD_DIM]
        qa_ref[hh, :, HEAD_DIM:] = mask_rows
    m_ref[...] = jnp.full(m_ref.shape, -jnp.inf, F32)
    acc_ref[...] = jnp.zeros(acc_ref.shape, F32)
    lane = lax.broadcasted_iota(jnp.int32, (blk, n_gate), 1)
    ones = jnp.ones((blk, HEAD_DIM), BF16)

    def scores(slot, hh, jb):
        st = pl.multiple_of(jb * blk, blk)
        col = jb * (n_gate // nb) + hp * ATTN_HEADS + hh
        onehot = jnp.where(lane == col, 1.0, 0.0).astype(BF16)
        kj = k_ref[0, pl.ds(st, blk), hh * HEAD_DIM:(hh + 1) * HEAD_DIM]
        s_ref[slot, hh] = lax.dot_general(qa_ref[hh], jnp.concatenate([kj, onehot], axis=1), _NT,
                                          preferred_element_type=F32)

    def update(slot, hh, jb, causal):
        s = s_ref[slot, hh]
        if causal:
            qpos = lax.broadcasted_iota(jnp.int32, s.shape, 0) + i * tq
            kpos = lax.broadcasted_iota(jnp.int32, s.shape, 1) + jb * blk
            s = jnp.where(kpos <= qpos, s, NEG)
        st = pl.multiple_of(jb * blk, blk)
        m_old = m_ref[hh]
        m_new = jnp.maximum(m_old, jnp.max(s, axis=-1, keepdims=True))
        alpha = jnp.exp(m_old - m_new)
        p = jnp.exp(s - jnp.concatenate([m_new] * (blk // LANES), axis=1))
        vj = v_ref[0, pl.ds(st, blk), hh * HEAD_DIM:(hh + 1) * HEAD_DIM]
        acc_ref[hh] = jnp.concatenate([alpha, alpha], axis=1) * acc_ref[hh] + jnp.dot(
            p.astype(BF16), jnp.concatenate([vj, ones], axis=1), preferred_element_type=F32)
        m_ref[hh] = m_new

    for hh in heads:
        scores(0, hh, 0)

    def pair(jj, carry):
        for hh in heads:
            scores(1, hh, 2 * jj + 1)
            update(0, hh, 2 * jj, False)
            scores(0, hh, 2 * jj + 2)
            update(1, hh, 2 * jj + 1, False)
        return carry

    lax.fori_loop(0, i, pair, 0)
    for hh in heads:
        scores(1, hh, 2 * i + 1)
        update(0, hh, 2 * i, True)
        update(1, hh, 2 * i + 1, True)
        o_ref[0, :, hh * HEAD_DIM:(hh + 1) * HEAD_DIM] = (
            acc_ref[hh, :, :HEAD_DIM] / acc_ref[hh, :, HEAD_DIM:]).astype(o_ref.dtype)


def _attention(proj3, bias, nb, d_ssm):
    bsz, s, _ = proj3.shape
    n_gate = bias.shape[2]
    n_heads = n_gate // nb
    d_attn = n_heads * HEAD_DIM
    q0 = d_ssm // HEAD_DIM
    k0 = q0 + n_heads
    v0 = k0 + n_heads
    assert ATTN_Q_BLOCKS == 2 and nb % ATTN_Q_BLOCKS == 0
    assert n_heads % ATTN_HEADS == 0 and q0 % ATTN_HEADS == 0
    tq = ATTN_Q_BLOCKS * MOBA_BLOCK
    hw = ATTN_HEADS * HEAD_DIM
    return pl.pallas_call(
        functools.partial(_attn_kernel, nb=nb),
        grid=(bsz, n_heads // ATTN_HEADS, s // tq),
        in_specs=[
            pl.BlockSpec((1, tq, hw), lambda b, h, i: (b, i, q0 // ATTN_HEADS + h)),
            pl.BlockSpec((1, s, hw), lambda b, h, i: (b, 0, k0 // ATTN_HEADS + h)),
            pl.BlockSpec((1, s, hw), lambda b, h, i: (b, 0, v0 // ATTN_HEADS + h)),
            pl.BlockSpec((1, tq, n_gate), lambda b, h, i: (b, i, 0)),
        ],
        out_specs=pl.BlockSpec((1, tq, hw), lambda b, h, i: (b, i, h)),
        out_shape=jax.ShapeDtypeStruct((bsz, s, d_attn), BF16),
        scratch_shapes=[
            pltpu.VMEM((ATTN_HEADS, tq, HEAD_DIM + n_gate), BF16),
            pltpu.VMEM((2, ATTN_HEADS, tq, MOBA_BLOCK), F32),
            pltpu.VMEM((ATTN_HEADS, tq, LANES), F32),
            pltpu.VMEM((ATTN_HEADS, tq, 2 * HEAD_DIM), F32),
        ],
        compiler_params=_params(3),
        name="moba_attn",
    )(proj3, proj3, proj3, bias)


def _ssm_tables(lam_re, lam_im, log_dt, b_re, b_im, c_re, c_im, d_skip):
    t_len = SSM_CHUNK
    n_groups, n_state = lam_re.shape
    n_ch = SSM_GROUP
    assert 2 * n_state == LANES
    w = t_len * n_ch
    dt = jnp.exp(log_dt)[:, None]
    ar = lam_re * dt
    ai = lam_im * dt
    steps = jnp.arange(t_len + 1, dtype=F32)[None, :, None]
    mag = jnp.exp(ar[:, None, :] * steps)
    pw_re = mag * jnp.cos(ai[:, None, :] * steps)
    pw_im = mag * jnp.sin(ai[:, None, :] * steps)
    e1 = jnp.expm1(ar)
    sh = jnp.sin(0.5 * ai)
    num_re = e1 * jnp.cos(ai) - 2.0 * sh * sh
    num_im = (e1 + 1.0) * jnp.sin(ai)
    den = lam_re * lam_re + lam_im * lam_im
    coef_re = (num_re * lam_re + num_im * lam_im) / den
    coef_im = (num_im * lam_re - num_re * lam_im) / den
    bb_re = coef_re[..., None] * b_re - coef_im[..., None] * b_im
    bb_im = coef_re[..., None] * b_im + coef_im[..., None] * b_re
    cp_re = c_re[:, None] * pw_re[:, :, None, :] - c_im[:, None] * pw_im[:, :, None, :]
    cp_im = c_re[:, None] * pw_im[:, :, None, :] + c_im[:, None] * pw_re[:, :, None, :]
    cpow = jnp.concatenate([cp_re, -cp_im], axis=-1).reshape(n_groups, (t_len + 1) * n_ch, LANES)
    lane_of = (jnp.arange(n_groups) % SSM_LANE_GROUPS)[:, None] * n_ch + jnp.arange(n_ch)[None, :]
    place = (lane_of[:, :, None] == jnp.arange(LANES)[None, None, :]).astype(F32)
    bshift = jnp.einsum('gpm,gml->gpl', jnp.concatenate([bb_re, bb_im], axis=1), place, precision=_HI)
    rev_re = pw_re[:, :t_len][:, ::-1][:, :, None, :]
    rev_im = pw_im[:, :t_len][:, ::-1][:, :, None, :]
    bt_re = bb_re.transpose(0, 2, 1)[:, None]
    bt_im = bb_im.transpose(0, 2, 1)[:, None]
    pin = jnp.stack([rev_re * bt_re - rev_im * bt_im, rev_re * bt_im + rev_im * bt_re], axis=1)
    pad_p = LANES - n_state
    pin = jnp.pad(pin.reshape(n_groups, 2, w, n_state), ((0, 0), (0, 0), (0, 0), (0, pad_p)))
    adec = jnp.stack([pw_re[:, t_len], pw_im[:, t_len]], axis=1)[:, :, None, :]
    adec = jnp.pad(adec, ((0, 0), (0, 0), (0, 0), (0, pad_p)))
    return cpow, bshift, pin.astype(BF16), adec


def _ssm_kernel(x_ref, cpow_ref, bshift_ref, pin_ref, adec_ref, dvec_ref, o_ref,
                rs_ref, xall_ref, hre_ref, him_ref, kk_ref, wt_ref, wrev_ref, pbd_ref, qbd_ref,
                *, bsz, nc):
    t_len = SSM_CHUNK
    n_pairs = t_len // 2
    ng = SSM_LANE_GROUPS
    gw = SSM_GROUP
    n_state = LANES // 2
    n_chunks = bsz * nc
    wide = ng * LANES

    for g in range(ng):
        kk_ref[g] = jnp.dot(cpow_ref[g, 0:t_len * gw, :], bshift_ref[g], preferred_element_type=F32,
                            precision=_HI)
    row = lax.broadcasted_iota(jnp.int32, (LANES, LANES), 0)
    col = lax.broadcasted_iota(jnp.int32, (LANES, LANES), 1)
    wt_ref[0] = jnp.zeros((LANES, LANES), BF16)
    for l in range(t_len):
        tile = kk_ref[:, l * gw:(l + 1) * gw, :].reshape(LANES, LANES)
        if l == 0:
            tile = tile + jnp.where(row == col, dvec_ref[0], 0.0)
        wt_ref[l + 1] = tile.astype(BF16)
    pw = 2 * LANES
    for d in range(n_pairs):
        c0 = (n_pairs - 1 - d) * pw
        wrev_ref[0:LANES, c0:c0 + LANES] = wt_ref[2 * d + 1]
        wrev_ref[0:LANES, c0 + LANES:c0 + pw] = wt_ref[2 * d]
        wrev_ref[LANES:, c0:c0 + LANES] = wt_ref[2 * d + 2]
        wrev_ref[LANES:, c0 + LANES:c0 + pw] = wt_ref[2 * d + 1]

    rs_ref[...] = x_ref[...].astype(F32)
    for s in range(t_len):
        xall_ref[:, s * LANES:(s + 1) * LANES] = rs_ref[pl.ds(s, n_chunks, stride=t_len), :].astype(BF16)

    pbd_ref[...] = jnp.zeros(pbd_ref.shape, BF16)
    for k in range(t_len // SSM_STATE_POS):
        for sl in range(SSM_STATE_POS):
            s0 = (k * SSM_STATE_POS + sl) * gw
            for g in range(ng):
                rows = slice(sl * LANES + g * gw, sl * LANES + (g + 1) * gw)
                lanes = slice(g * LANES, (g + 1) * LANES)
                pbd_ref[0, rows, lanes] = pin_ref[g, 0, s0:s0 + gw, :]
                pbd_ref[1, rows, lanes] = pin_ref[g, 1, s0:s0 + gw, :]
        xk = xall_ref[:, k * SSM_STATE_POS * LANES:(k + 1) * SSM_STATE_POS * LANES]
        h_re = jnp.dot(xk, pbd_ref[0], preferred_element_type=F32)
        h_im = jnp.dot(xk, pbd_ref[1], preferred_element_type=F32)
        if k == 0:
            hre_ref[...] = h_re
            him_ref[...] = h_im
        else:
            hre_ref[...] += h_re
            him_ref[...] += h_im

    a_re = jnp.concatenate([adec_ref[g, 0] for g in range(ng)], axis=1)
    a_im = jnp.concatenate([adec_ref[g, 1] for g in range(ng)], axis=1)

    def step(c, carry):
        new = []
        for b in range(bsz):
            s_re, s_im = carry[b]
            r = b * nc + c
            loc_re = hre_ref[pl.ds(r, 1), :]
            loc_im = him_ref[pl.ds(r, 1), :]
            hre_ref[pl.ds(r, 1), :] = s_re
            him_ref[pl.ds(r, 1), :] = s_im
            new.append((a_re * s_re - a_im * s_im + loc_re,
                        a_re * s_im + a_im * s_re + loc_im))
        return tuple(new)

    zero = jnp.zeros((1, wide), F32)
    lax.fori_loop(0, nc, step, tuple((zero, zero) for _ in range(bsz)), unroll=2)

    h_in = (hre_ref[...] + pltpu.roll(him_ref[...], n_state, 1)).astype(BF16)
    qbd_ref[...] = jnp.zeros(qbd_ref.shape, BF16)
    for q in range(n_pairs):
        for tl in range(2):
            s0 = (2 * q + tl + 1) * gw
            for g in range(ng):
                qbd_ref[tl * LANES + g * gw:tl * LANES + (g + 1) * gw, g * LANES:(g + 1) * LANES] = (
                    cpow_ref[g, s0:s0 + gw, :].astype(BF16))
        y = lax.dot_general(h_in, qbd_ref[...], _NT, preferred_element_type=F32)
        y = y + lax.dot_general(xall_ref[:, 0:(q + 1) * pw], wrev_ref[:, (n_pairs - 1 - q) * pw:], _NT,
                                preferred_element_type=F32)
        y = jax.nn.gelu(y)
        rs_ref[pl.ds(2 * q, n_chunks, stride=t_len), :] = y[:, 0:LANES]
        rs_ref[pl.ds(2 * q + 1, n_chunks, stride=t_len), :] = y[:, LANES:]
    o_ref[...] = rs_ref[...].astype(o_ref.dtype)


def _ssm(proj2, tables, dvec, layer, bsz, d_ssm):
    cpow, bshift, pin, adec = tables
    rows = proj2.shape[0]
    nc = rows // bsz // SSM_CHUNK
    n_chunks = bsz * nc
    ng = SSM_LANE_GROUPS
    n_pairs = SSM_CHUNK // 2
    wide = ng * LANES
    cp_rows = cpow.shape[2]
    w = SSM_CHUNK * SSM_GROUP
    return pl.pallas_call(
        functools.partial(_ssm_kernel, bsz=bsz, nc=nc),
        grid=(d_ssm // LANES,),
        in_specs=[
            pl.BlockSpec((rows, LANES), lambda j: (0, j), pipeline_mode=pl.Buffered(1)),
            pl.BlockSpec((None, ng, cp_rows, LANES), lambda j: (layer, j, 0, 0)),
            pl.BlockSpec((None, ng, LANES, LANES), lambda j: (layer, j, 0, 0)),
            pl.BlockSpec((None, ng, 2, w, LANES), lambda j: (layer, j, 0, 0, 0)),
            pl.BlockSpec((None, ng, 2, 1, LANES), lambda j: (layer, j, 0, 0, 0)),
            pl.BlockSpec((None, 1, 1, LANES), lambda j: (layer, j, 0, 0)),
        ],
        out_specs=pl.BlockSpec((rows, LANES), lambda j: (0, j)),
        out_shape=jax.ShapeDtypeStruct((rows, d_ssm), BF16),
        scratch_shapes=[
            pltpu.VMEM((rows, LANES), F32),
            pltpu.VMEM((n_chunks, SSM_CHUNK * LANES), BF16),
            pltpu.VMEM((n_chunks, wide), F32),
            pltpu.VMEM((n_chunks, wide), F32),
            pltpu.VMEM((ng, w, LANES), F32),
            pltpu.VMEM((SSM_CHUNK + 1, LANES, LANES), BF16),
            pltpu.VMEM((2 * LANES, n_pairs * 2 * LANES), BF16),
            pltpu.VMEM((2, SSM_STATE_POS * LANES, wide), BF16),
            pltpu.VMEM((2 * LANES, wide), BF16),
        ],
        compiler_params=_params(1),
        name="s5_chunked",
    )(proj2, cpow, bshift, pin, adec, dvec)


def _merge_kernel(y_ref, att_ref, ga_ref, gb_ref, x_ref, wglu_ref, wus_ref, wua_ref, wout_ref,
                  g_ref, o_ref):
    tm = x_ref.shape[0]
    strip = min(MERGE_STRIP, tm)
    for r0 in range(0, tm, strip):
        rows = slice(r0, r0 + strip)
        y = y_ref[rows, :]
        z = jnp.dot(y, wglu_ref[...], preferred_element_type=F32)
        ya = jnp.dot(att_ref[rows, :], wua_ref[...], preferred_element_type=F32)
        s5 = (y.astype(F32) * jax.nn.sigmoid(z)).astype(BF16)
        ys = jnp.dot(s5, wus_ref[...], preferred_element_type=F32)
        m = (jax.nn.sigmoid(ga_ref[rows, :].astype(F32)) * ys
             + jax.nn.sigmoid(gb_ref[rows, :].astype(F32)) * ya)
        o = jnp.dot(m.astype(BF16), wout_ref[...], preferred_element_type=F32)
        o_ref[rows, :] = x_ref[rows, :] + _rmsnorm_rows(o, g_ref[...])


def _merge(y2, att2, proj2, x2, w_glu, w_us, w_ua, w_out, layer, g_post):
    n_rows, d = x2.shape
    d_ssm = y2.shape[1]
    d_attn = att2.shape[1]
    tm = min(ROW_TILE_MERGE, n_rows)
    ga_blk = (d_ssm + 3 * d_attn) // d

    def weight(wt):
        return pl.BlockSpec((None,) + wt.shape[1:], lambda i: (layer, 0, 0),
                            pipeline_mode=pl.Buffered(1))

    return pl.pallas_call(
        _merge_kernel,
        grid=(n_rows // tm,),
        in_specs=[
            pl.BlockSpec((tm, d_ssm), lambda i: (i, 0)),
            pl.BlockSpec((tm, d_attn), lambda i: (i, 0)),
            pl.BlockSpec((tm, d), lambda i: (i, ga_blk)),
            pl.BlockSpec((tm, d), lambda i: (i, ga_blk + 1)),
            pl.BlockSpec((tm, d), lambda i: (i, 0)),
            weight(w_glu), weight(w_us), weight(w_ua), weight(w_out),
            pl.BlockSpec((1, d), lambda i: (0, 0)),
        ],
        out_specs=pl.BlockSpec((tm, d), lambda i: (i, 0)),
        out_shape=jax.ShapeDtypeStruct((n_rows, d), F32),
        compiler_params=_params(1),
        name="merge",
    )(y2, att2, proj2, proj2, x2, w_glu, w_us, w_ua, w_out, g_post)


def _ffn_kernel(x_ref, gpre_ref, wa_ref, wv_ref, cwa_ref, cwv_ref, cba_ref, cbv_ref, wd_ref,
                gpost_ref, o_ref, h_ref, za_ref, zv_ref, halo_a_ref, halo_v_ref,
                *, tiles_per_seq):
    i = pl.program_id(0)
    c = pl.program_id(1)
    tm = x_ref.shape[0]

    @pl.when(c == 0)
    def _():
        _rmsnorm_into(h_ref, x_ref, gpre_ref)
        o_ref[...] = jnp.zeros_like(o_ref)

    @pl.when((i % tiles_per_seq) == 0)
    def _():
        halo_a_ref[c] = jnp.zeros(halo_a_ref.shape[1:], F32)
        halo_v_ref[c] = jnp.zeros(halo_v_ref.shape[1:], F32)

    za_ref[0:SUBLANES, :] = halo_a_ref[c]
    zv_ref[0:SUBLANES, :] = halo_v_ref[c]
    cwa, cwv = cwa_ref[...], cwv_ref[...]
    cba, cbv = cba_ref[...], cbv_ref[...]

    def up(r0):
        h = h_ref[r0:r0 + FFN_STRIP, :]
        out_rows = slice(SUBLANES + r0, SUBLANES + r0 + FFN_STRIP)
        za_ref[out_rows, :] = jnp.dot(h, wa_ref[...], preferred_element_type=F32)
        zv_ref[out_rows, :] = jnp.dot(h, wv_ref[...], preferred_element_type=F32)

    def conv_strip(z_ref, cw, bias, r0):
        base = SUBLANES + r0
        z0 = z_ref[base:base + FFN_STRIP, :]
        z1 = z_ref[base - 1:base - 1 + FFN_STRIP, :]
        z2 = z_ref[base - 2:base - 2 + FFN_STRIP, :]
        return cw[0:1, :] * z2 + cw[1:2, :] * z1 + cw[2:3, :] * z0 + bias

    def finish(r0):
        a = conv_strip(za_ref, cwa, cba, r0)
        v = conv_strip(zv_ref, cwv, cbv, r0)
        gated = (a * jax.nn.sigmoid(a) * v).astype(BF16)
        o_ref[r0:r0 + FFN_STRIP, :] += jnp.dot(gated, wd_ref[...], preferred_element_type=F32)

    strips = list(range(0, tm, FFN_STRIP))
    up(strips[0])
    for k, r0 in enumerate(strips):
        if k + 1 < len(strips):
            up(strips[k + 1])
        finish(r0)

    halo_a_ref[c] = za_ref[tm:tm + SUBLANES, :]
    halo_v_ref[c] = zv_ref[tm:tm + SUBLANES, :]

    @pl.when(c == pl.num_programs(1) - 1)
    def _():
        step_rows = min(NORM_ROWS, tm)

        def finalize(r, carry):
            rows = pl.ds(pl.multiple_of(r * step_rows, step_rows), step_rows)
            o_ref[rows, :] = x_ref[rows, :] + _rmsnorm_rows(o_ref[rows, :], gpost_ref[...])
            return carry

        lax.fori_loop(0, tm // step_rows, finalize, 0)


def _ffn(x2, g_pre, w_up, conv_w, conv_b, w_down, layer, g_post, seq_len):
    n_rows, d = x2.shape
    d_ff = w_down.shape[1]
    tm = min(ROW_TILE_FFN, seq_len)
    tf = COL_TILE_FFN
    nff = d_ff // tf
    assert tm % FFN_STRIP == 0
    const = lambda i, c: (0, 0)
    zbuf = pltpu.VMEM((SUBLANES + tm, tf), F32)
    return pl.pallas_call(
        functools.partial(_ffn_kernel, tiles_per_seq=seq_len // tm),
        grid=(n_rows // tm, nff),
        in_specs=[
            pl.BlockSpec((tm, d), lambda i, c: (i, 0)),
            pl.BlockSpec((1, d), const),
            pl.BlockSpec((None, d, tf), lambda i, c: (layer, 0, c)),
            pl.BlockSpec((None, d, tf), lambda i, c: (layer, 0, nff + c)),
            pl.BlockSpec((None, CONV_W, tf), lambda i, c: (layer, 0, c)),
            pl.BlockSpec((None, CONV_W, tf), lambda i, c: (layer, 0, nff + c)),
            pl.BlockSpec((None, 1, tf), lambda i, c: (layer, 0, c)),
            pl.BlockSpec((None, 1, tf), lambda i, c: (layer, 0, nff + c)),
            pl.BlockSpec((None, tf, d), lambda i, c: (layer, c, 0)),
            pl.BlockSpec((1, d), const),
        ],
        out_specs=pl.BlockSpec((tm, d), lambda i, c: (i, 0)),
        out_shape=jax.ShapeDtypeStruct((n_rows, d), F32),
        scratch_shapes=[
            pltpu.VMEM((tm, d), BF16),
            zbuf, zbuf,
            pltpu.VMEM((nff, SUBLANES, tf), F32),
            pltpu.VMEM((nff, SUBLANES, tf), F32),
        ],
        compiler_params=_params(2),
        name="convglu_ffn",
    )(x2, g_pre, w_up, w_up, conv_w, conv_w, conv_b, conv_b, w_down, g_post)


def kernel(x, g_pre_mix, w_in, lam_re, lam_im, log_dt, b_re, b_im, c_re, c_im, d_skip, w_glu,
           w_up_ssm, w_up_attn, w_out, g_post_mix, g_pre_ffn, w_ffn_up, conv_w, conv_b,
           w_ffn_down, g_post_ffn):
    bsz, seq_len, d = x.shape
    depth = w_in.shape[0]
    d_ssm = w_glu.shape[1]
    d_attn = w_up_attn.shape[1]
    n_heads = d_attn // HEAD_DIM
    n_groups = d_ssm // SSM_GROUP
    assert seq_len % (ATTN_Q_BLOCKS * MOBA_BLOCK) == 0 and seq_len % SSM_CHUNK == 0
    assert d_ssm % d_attn == 0
    nb = seq_len // MOBA_BLOCK
    nc = seq_len // SSM_CHUNK
    n_rows = bsz * seq_len
    row = lambda v: v.reshape(1, -1)

    w_in_bf = w_in.astype(BF16)
    w_glu_bf = w_glu.astype(BF16)
    w_us_bf = w_up_ssm.astype(BF16)
    w_ua_bf = w_up_attn.astype(BF16)
    w_out_bf = w_out.astype(BF16)
    w_fu_bf = w_ffn_up.astype(BF16)
    w_fd_bf = w_ffn_down.astype(BF16)
    conv_b3 = conv_b[:, None, :]
    tables = jax.vmap(_ssm_tables)(lam_re, lam_im, log_dt, b_re, b_im, c_re, c_im, d_skip)
    dvec = d_skip.reshape(depth, d_ssm // LANES, 1, LANES)
    col_scale = jnp.ones((w_in.shape[2],), F32).at[d_ssm:d_ssm + d_attn].set(1.0 / math.sqrt(HEAD_DIM))

    x2 = x.reshape(n_rows, d)
    for l in range(depth):
        proj2, hbar = _inproj(x2, row(g_pre_mix[l]), w_in_bf, l, row(col_scale))

        hbar = hbar.reshape(bsz, nb, 1, d)
        hbar_t = jnp.broadcast_to(hbar, (bsz, nb, n_heads, d)).reshape(bsz * nb * n_heads, d)
        kbd = _kmean(hbar_t, w_in, l, nb, d_ssm, d_attn).reshape(bsz, n_heads * nb, d_attn)
        z = _gatevec(w_in, l, kbd, d_ssm)
        bias = _select(x2.reshape(bsz, seq_len, d), row(g_pre_mix[l]), z, nb)
        att = _attention(proj2.reshape(bsz, seq_len, -1), bias, nb, d_ssm)

        y2 = _ssm(proj2, tables, dvec, l, bsz, d_ssm)

        x2 = _merge(y2, att.reshape(n_rows, d_attn), proj2, x2, w_glu_bf, w_us_bf, w_ua_bf,
                    w_out_bf, l, row(g_post_mix[l]))
        x2 = _ffn(x2, row(g_pre_ffn[l]), w_fu_bf, conv_w, conv_b3, w_fd_bf, l,
                  row(g_post_ffn[l]), seq_len)
    return x2.reshape(bsz, seq_len, d)
```

```python
import functools
import math

import jax
import jax.numpy as jnp
from jax import lax
from jax.experimental import pallas as pl
from jax.experimental.pallas import tpu as pltpu

F32 = jnp.float32
BF16 = jnp.bfloat16

EPS = 1e-6
NEG = -1e30
SSM_GROUP = 16
HEAD_DIM = 128
MOBA_BLOCK = 256
MOBA_TOPK = 3
CONV_W = 3

LANES = 128
SUBLANES = 8
VMEM_LIMIT_BYTES = 56 * 1024 * 1024

SSM_CHUNK = 32
SSM_LANE_GROUPS = LANES // SSM_GROUP
SSM_STATE_POS = 8
ROW_TILE_PROJ = 1024
COL_TILE_PROJ = 2048
ROW_TILE_GATE = 512
ROW_TILE_MERGE = 512
MERGE_STRIP = 256
ROW_TILE_FFN = 512
COL_TILE_FFN = 512
FFN_STRIP = 256
K_TILE_GATE = 512
NORM_ROWS = 128
ATTN_Q_BLOCKS = 2
ATTN_HEADS = 2

_NT = (((1,), (1,)), ((), ()))
_HI = lax.Precision.HIGHEST


def _params(n_axes):
    return pltpu.CompilerParams(
        dimension_semantics=("arbitrary",) * n_axes,
        vmem_limit_bytes=VMEM_LIMIT_BYTES,
    )


def _rmsnorm_rows(x, g):
    ms = jnp.mean(x * x, axis=-1, keepdims=True)
    return x * lax.rsqrt(ms + EPS) * g


def _rmsnorm_into(h_ref, x_ref, g_ref):
    step_rows = min(NORM_ROWS, x_ref.shape[0])

    def step(r, carry):
        r0 = pl.multiple_of(r * step_rows, step_rows)
        h_ref[pl.ds(r0, step_rows), :] = _rmsnorm_rows(
            x_ref[pl.ds(r0, step_rows), :], g_ref[...]).astype(h_ref.dtype)
        return carry

    lax.fori_loop(0, x_ref.shape[0] // step_rows, step, 0)


def _inproj_kernel(x_ref, g_ref, w_ref, cs_ref, o_ref, hb_ref, h_ref):
    @pl.when(pl.program_id(1) == 0)
    def _():
        def step(r, carry):
            r0 = pl.multiple_of(r * MOBA_BLOCK, MOBA_BLOCK)
            hn = _rmsnorm_rows(x_ref[pl.ds(r0, MOBA_BLOCK), :], g_ref[...])
            h_ref[pl.ds(r0, MOBA_BLOCK), :] = hn.astype(h_ref.dtype)
            hb_ref[r] = jnp.mean(hn, axis=0, keepdims=True)
            return carry

        lax.fori_loop(0, x_ref.shape[0] // MOBA_BLOCK, step, 0)

    acc = jnp.dot(h_ref[...], w_ref[...], preferred_element_type=F32)
    o_ref[...] = (acc * cs_ref[...]).astype(o_ref.dtype)


def _inproj(x2, g, w_bf, layer, col_scale):
    n_rows, d = x2.shape
    d_in = w_bf.shape[2]
    tm = min(ROW_TILE_PROJ, n_rows)
    tn = COL_TILE_PROJ
    assert tm % MOBA_BLOCK == 0
    blocks_per_tile = tm // MOBA_BLOCK
    return pl.pallas_call(
        _inproj_kernel,
        grid=(n_rows // tm, d_in // tn),
        in_specs=[
            pl.BlockSpec((tm, d), lambda i, j: (i, 0)),
            pl.BlockSpec((1, d), lambda i, j: (0, 0)),
            pl.BlockSpec((None, d, tn), lambda i, j: (layer, 0, j)),
            pl.BlockSpec((1, tn), lambda i, j: (0, j)),
        ],
        out_specs=[
            pl.BlockSpec((tm, tn), lambda i, j: (i, j)),
            pl.BlockSpec((blocks_per_tile, 1, d), lambda i, j: (i, 0, 0)),
        ],
        out_shape=[
            jax.ShapeDtypeStruct((n_rows, d_in), BF16),
            jax.ShapeDtypeStruct((n_rows // MOBA_BLOCK, 1, d), F32),
        ],
        scratch_shapes=[pltpu.VMEM((tm, d), BF16)],
        compiler_params=_params(2),
        name="inproj",
    )(x2, g, w_bf, col_scale)


def _kmean_kernel(hb_ref, wk_ref, o_ref, *, nb):
    @pl.when(pl.program_id(0) == 0)
    def _():
        o_ref[...] = jnp.zeros_like(o_ref)

    o_ref[...] += jnp.dot(hb_ref[...], wk_ref[...], preferred_element_type=F32, precision=_HI)

    @pl.when(pl.program_id(0) == pl.num_programs(0) - 1)
    def _():
        r = lax.broadcasted_iota(jnp.int32, o_ref.shape, 0)
        c = lax.broadcasted_iota(jnp.int32, o_ref.shape, 1)
        n_heads = o_ref.shape[1] // HEAD_DIM
        keep = (c // HEAD_DIM) == (r % n_heads)
        o_ref[...] = jnp.where(keep, o_ref[...], 0.0)


def _kmean(hbar_t, w_in, layer, nb, d_ssm, d_attn):
    rows, d = hbar_t.shape
    k_col_block = (d_ssm + d_attn) // d_attn
    tk = K_TILE_GATE
    return pl.pallas_call(
        functools.partial(_kmean_kernel, nb=nb),
        grid=(d // tk,),
        in_specs=[
            pl.BlockSpec((rows, tk), lambda kk: (0, kk)),
            pl.BlockSpec((None, tk, d_attn), lambda kk: (layer, kk, k_col_block)),
        ],
        out_specs=pl.BlockSpec((rows, d_attn), lambda kk: (0, 0)),
        out_shape=jax.ShapeDtypeStruct((rows, d_attn), F32),
        compiler_params=_params(1),
        name="kmean",
    )(hbar_t, w_in)


def _gatevec_kernel(wq_ref, kbd_ref, o_ref):
    o_ref[0] = lax.dot_general(wq_ref[...], kbd_ref[0], _NT, preferred_element_type=F32,
                               precision=_HI)


def _gatevec(w_in, layer, kbd, d_ssm):
    bsz, n_gate, d_attn = kbd.shape
    d = w_in.shape[1]
    q_col_block = d_ssm // d_attn
    tk = K_TILE_GATE
    return pl.pallas_call(
        _gatevec_kernel,
        grid=(bsz, d // tk),
        in_specs=[
            pl.BlockSpec((None, tk, d_attn), lambda b, kk: (layer, kk, q_col_block)),
            pl.BlockSpec((1, n_gate, d_attn), lambda b, kk: (b, 0, 0)),
        ],
        out_specs=pl.BlockSpec((1, tk, n_gate), lambda b, kk: (b, kk, 0)),
        out_shape=jax.ShapeDtypeStruct((bsz, d, n_gate), F32),
        compiler_params=_params(2),
        name="gatevec",
    )(w_in, kbd)


def _select_kernel(x_ref, g_ref, z_ref, o_ref, *, nb, tq):
    h = _rmsnorm_rows(x_ref[0], g_ref[...])
    gate = jnp.dot(h, z_ref[0], preferred_element_type=F32, precision=_HI)
    n_heads = gate.shape[1] // nb
    row = lax.broadcasted_iota(jnp.int32, gate.shape, 0) + pl.program_id(1) * tq
    qblk = row // MOBA_BLOCK
    j = lax.broadcasted_iota(jnp.int32, gate.shape, 1) // n_heads
    past = j < qblk
    gate = jnp.where(past, gate, NEG)
    cnt = jnp.zeros(gate.shape, F32)
    for r in range(1, nb):
        other = pltpu.roll(gate, r * n_heads, 1)
        wins_tie = jnp.where(other >= gate, 1.0, 0.0)
        wins_strict = jnp.where(other > gate, 1.0, 0.0)
        cnt = cnt + jnp.where(j >= r, wins_tie, wins_strict)
    keep = jnp.logical_or(jnp.logical_and(past, cnt < float(MOBA_TOPK)), j == qblk)
    o_ref[0] = jnp.where(keep, 0.0, NEG)


def _select(x3, g, z, nb):
    bsz, s, d = x3.shape
    n_gate = z.shape[2]
    tq = min(ROW_TILE_GATE, s)
    return pl.pallas_call(
        functools.partial(_select_kernel, nb=nb, tq=tq),
        grid=(bsz, s // tq),
        in_specs=[
            pl.BlockSpec((1, tq, d), lambda b, i: (b, i, 0)),
            pl.BlockSpec((1, d), lambda b, i: (0, 0)),
            pl.BlockSpec((1, d, n_gate), lambda b, i: (b, 0, 0)),
        ],
        out_specs=pl.BlockSpec((1, tq, n_gate), lambda b, i: (b, i, 0)),
        out_shape=jax.ShapeDtypeStruct((bsz, s, n_gate), F32),
        compiler_params=_params(2),
        name="select",
    )(x3, g, z)


def _attn_kernel(q_ref, k_ref, v_ref, bias_ref, o_ref, qa_ref, s_ref, m_ref, acc_ref, *, nb):
    hp = pl.program_id(1)
    i = pl.program_id(2)
    blk = MOBA_BLOCK
    tq = q_ref.shape[1]
    n_gate = bias_ref.shape[2]
    heads = range(ATTN_HEADS)
    mask_rows = bias_ref[0].astype(BF16)
    for hh in heads:
        qa_ref[hh, :, :HEAD_DIM] = q_ref[0, :, hh * HEAD_DIM:(hh + 1) * HEAD_DIM]
        qa_ref[hh, :, HEAD_DIM:] = mask_rows
    m_ref[...] = jnp.full(m_ref.shape, -jnp.inf, F32)
    acc_ref[...] = jnp.zeros(acc_ref.shape, F32)
    lane = lax.broadcasted_iota(jnp.int32, (blk, n_gate), 1)
    ones = jnp.ones((blk, HEAD_DIM), BF16)

    def scores(slot, hh, jb):
        st = pl.multiple_of(jb * blk, blk)
        col = jb * (n_gate // nb) + hp * ATTN_HEADS + hh
        onehot = jnp.where(lane == col, 1.0, 0.0).astype(BF16)
        kj = k_ref[0, pl.ds(st, blk), hh * HEAD_DIM:(hh + 1) * HEAD_DIM]
        s_ref[slot, hh] = lax.dot_general(qa_ref[hh], jnp.concatenate([kj, onehot], axis=1), _NT,
                                          preferred_element_type=F32)

    def update(slot, hh, jb, causal):
        s = s_ref[slot, hh]
        if causal:
            qpos = lax.broadcasted_iota(jnp.int32, s.shape, 0) + i * tq
            kpos = lax.broadcasted_iota(jnp.int32, s.shape, 1) + jb * blk
            s = jnp.where(kpos <= qpos, s, NEG)
        st = pl.multiple_of(jb * blk, blk)
        m_old = m_ref[hh]
        m_new = jnp.maximum(m_old, jnp.max(s, axis=-1, keepdims=True))
        alpha = jnp.exp(m_old - m_new)
        p = jnp.exp(s - jnp.concatenate([m_new] * (blk // LANES), axis=1))
        vj = v_ref[0, pl.ds(st, blk), hh * HEAD_DIM:(hh + 1) * HEAD_DIM]
        acc_ref[hh] = jnp.concatenate([alpha, alpha], axis=1) * acc_ref[hh] + jnp.dot(
            p.astype(BF16), jnp.concatenate([vj, ones], axis=1), preferred_element_type=F32)
        m_ref[hh] = m_new

    for hh in heads:
        scores(0, hh, 0)

    def pair(jj, carry):
        for hh in heads:
            scores(1, hh, 2 * jj + 1)
            update(0, hh, 2 * jj, False)
            scores(0, hh, 2 * jj + 2)
            update(1, hh, 2 * jj + 1, False)
        return carry

    lax.fori_loop(0, i, pair, 0)
    for hh in heads:
        scores(1, hh, 2 * i + 1)
        update(0, hh, 2 * i, True)
        update(1, hh, 2 * i + 1, True)
        o_ref[0, :, hh * HEAD_DIM:(hh + 1) * HEAD_DIM] = (
            acc_ref[hh, :, :HEAD_DIM] / acc_ref[hh, :, HEAD_DIM:]).astype(o_ref.dtype)


def _attention(proj3, bias, nb, d_ssm):
    bsz, s, _ = proj3.shape
    n_gate = bias.shape[2]
    n_heads = n_gate // nb
    d_attn = n_heads * HEAD_DIM
    q0 = d_ssm // HEAD_DIM
    k0 = q0 + n_heads
    v0 = k0 + n_heads
    assert ATTN_Q_BLOCKS == 2 and nb % ATTN_Q_BLOCKS == 0
    assert n_heads % ATTN_HEADS == 0 and q0 % ATTN_HEADS == 0
    tq = ATTN_Q_BLOCKS * MOBA_BLOCK
    hw = ATTN_HEADS * HEAD_DIM
    return pl.pallas_call(
        functools.partial(_attn_kernel, nb=nb),
        grid=(bsz, n_heads // ATTN_HEADS, s // tq),
        in_specs=[
            pl.BlockSpec((1, tq, hw), lambda b, h, i: (b, i, q0 // ATTN_HEADS + h)),
            pl.BlockSpec((1, s, hw), lambda b, h, i: (b, 0, k0 // ATTN_HEADS + h)),
            pl.BlockSpec((1, s, hw), lambda b, h, i: (b, 0, v0 // ATTN_HEADS + h)),
            pl.BlockSpec((1, tq, n_gate), lambda b, h, i: (b, i, 0)),
        ],
        out_specs=pl.BlockSpec((1, tq, hw), lambda b, h, i: (b, i, h)),
        out_shape=jax.ShapeDtypeStruct((bsz, s, d_attn), BF16),
        scratch_shapes=[
            pltpu.VMEM((ATTN_HEADS, tq, HEAD_DIM + n_gate), BF16),
            pltpu.VMEM((2, ATTN_HEADS, tq, MOBA_BLOCK), F32),
            pltpu.VMEM((ATTN_HEADS, tq, LANES), F32),
            pltpu.VMEM((ATTN_HEADS, tq, 2 * HEAD_DIM), F32),
        ],
        compiler_params=_params(3),
        name="moba_attn",
    )(proj3, proj3, proj3, bias)


def _ssm_tables(lam_re, lam_im, log_dt, b_re, b_im, c_re, c_im, d_skip):
    t_len = SSM_CHUNK
    n_groups, n_state = lam_re.shape
    n_ch = SSM_GROUP
    assert 2 * n_state == LANES
    w = t_len * n_ch
    dt = jnp.exp(log_dt)[:, None]
    ar = lam_re * dt
    ai = lam_im * dt
    steps = jnp.arange(t_len + 1, dtype=F32)[None, :, None]
    mag = jnp.exp(ar[:, None, :] * steps)
    pw_re = mag * jnp.cos(ai[:, None, :] * steps)
    pw_im = mag * jnp.sin(ai[:, None, :] * steps)
    e1 = jnp.expm1(ar)
    sh = jnp.sin(0.5 * ai)
    num_re = e1 * jnp.cos(ai) - 2.0 * sh * sh
    num_im = (e1 + 1.0) * jnp.sin(ai)
    den = lam_re * lam_re + lam_im * lam_im
    coef_re = (num_re * lam_re + num_im * lam_im) / den
    coef_im = (num_im * lam_re - num_re * lam_im) / den
    bb_re = coef_re[..., None] * b_re - coef_im[..., None] * b_im
    bb_im = coef_re[..., None] * b_im + coef_im[..., None] * b_re
    cp_re = c_re[:, None] * pw_re[:, :, None, :] - c_im[:, None] * pw_im[:, :, None, :]
    cp_im = c_re[:, None] * pw_im[:, :, None, :] + c_im[:, None] * pw_re[:, :, None, :]
    cpow = jnp.concatenate([cp_re, -cp_im], axis=-1).reshape(n_groups, (t_len + 1) * n_ch, LANES)
    lane_of = (jnp.arange(n_groups) % SSM_LANE_GROUPS)[:, None] * n_ch + jnp.arange(n_ch)[None, :]
    place = (lane_of[:, :, None] == jnp.arange(LANES)[None, None, :]).astype(F32)
    bshift = jnp.einsum('gpm,gml->gpl', jnp.concatenate([bb_re, bb_im], axis=1), place, precision=_HI)
    rev_re = pw_re[:, :t_len][:, ::-1][:, :, None, :]
    rev_im = pw_im[:, :t_len][:, ::-1][:, :, None, :]
    bt_re = bb_re.transpose(0, 2, 1)[:, None]
    bt_im = bb_im.transpose(0, 2, 1)[:, None]
    pin = jnp.concatenate([rev_re * bt_re - rev_im * bt_im, rev_re * bt_im + rev_im * bt_re], axis=-1)
    pad_p = LANES - n_state
    pin = pin.reshape(n_groups, w, LANES)
    adec = jnp.stack([pw_re[:, t_len], pw_im[:, t_len]], axis=1)[:, :, None, :]
    adec = jnp.pad(adec, ((0, 0), (0, 0), (0, 0), (0, pad_p)))
    return cpow, bshift, pin.astype(BF16), adec


def _ssm_kernel(x_ref, cpow_ref, bshift_ref, pin_ref, adec_ref, dvec_ref, o_ref,
                rs_ref, xall_ref, hre_ref, him_ref, kk_ref, wt_ref, wrev_ref, pbd_ref, qbd_ref,
                *, bsz, nc):
    t_len = SSM_CHUNK
    n_pairs = t_len // 2
    ng = SSM_LANE_GROUPS
    gw = SSM_GROUP
    n_state = LANES // 2
    n_chunks = bsz * nc
    wide = ng * LANES

    for g in range(ng):
        kk_ref[g] = jnp.dot(cpow_ref[g, 0:t_len * gw, :], bshift_ref[g], preferred_element_type=F32,
                            precision=_HI)
    row = lax.broadcasted_iota(jnp.int32, (LANES, LANES), 0)
    col = lax.broadcasted_iota(jnp.int32, (LANES, LANES), 1)
    wt_ref[0] = jnp.zeros((LANES, LANES), BF16)
    for l in range(t_len):
        tile = kk_ref[:, l * gw:(l + 1) * gw, :].reshape(LANES, LANES)
        if l == 0:
            tile = tile + jnp.where(row == col, dvec_ref[0], 0.0)
        wt_ref[l + 1] = tile.astype(BF16)
    pw = 2 * LANES
    for d in range(n_pairs):
        c0 = (n_pairs - 1 - d) * pw
        wrev_ref[0:LANES, c0:c0 + LANES] = wt_ref[2 * d + 1]
        wrev_ref[0:LANES, c0 + LANES:c0 + pw] = wt_ref[2 * d]
        wrev_ref[LANES:, c0:c0 + LANES] = wt_ref[2 * d + 2]
        wrev_ref[LANES:, c0 + LANES:c0 + pw] = wt_ref[2 * d + 1]

    rs_ref[...] = x_ref[...].astype(F32)
    for s in range(t_len):
        xall_ref[:, s * LANES:(s + 1) * LANES] = rs_ref[pl.ds(s, n_chunks, stride=t_len), :].astype(BF16)

    pbd_ref[...] = jnp.zeros(pbd_ref.shape, BF16)
    for k in range(t_len // SSM_STATE_POS):
        for sl in range(SSM_STATE_POS):
            s0 = (k * SSM_STATE_POS + sl) * gw
            for g in range(ng):
                rows = slice(sl * LANES + g * gw, sl * LANES + (g + 1) * gw)
                lanes = slice(g * LANES, (g + 1) * LANES)
                pbd_ref[rows, lanes] = pin_ref[g, s0:s0 + gw, :]
        xk = xall_ref[:, k * SSM_STATE_POS * LANES:(k + 1) * SSM_STATE_POS * LANES]
        h_cat = jnp.dot(xk, pbd_ref[...], preferred_element_type=F32)
        if k == 0:
            hre_ref[...] = h_cat
        else:
            hre_ref[...] += h_cat
    lower = (lax.broadcasted_iota(jnp.int32, (n_chunks, wide), 1) % LANES) < n_state
    h_cat = hre_ref[...]
    him_ref[...] = jnp.where(lower, pltpu.roll(h_cat, wide - n_state, 1), 0.0)
    hre_ref[...] = jnp.where(lower, h_cat, 0.0)

    a_re = jnp.concatenate([adec_ref[g, 0] for g in range(ng)], axis=1)
    a_im = jnp.concatenate([adec_ref[g, 1] for g in range(ng)], axis=1)

    def step(c, carry):
        new = []
        for b in range(bsz):
            s_re, s_im = carry[b]
            r = b * nc + c
            loc_re = hre_ref[pl.ds(r, 1), :]
            loc_im = him_ref[pl.ds(r, 1), :]
            hre_ref[pl.ds(r, 1), :] = s_re
            him_ref[pl.ds(r, 1), :] = s_im
            new.append((a_re * s_re - a_im * s_im + loc_re,
                        a_re * s_im + a_im * s_re + loc_im))
        return tuple(new)

    zero = jnp.zeros((1, wide), F32)
    lax.fori_loop(0, nc, step, tuple((zero, zero) for _ in range(bsz)), unroll=2)

    h_in = (hre_ref[...] + pltpu.roll(him_ref[...], n_state, 1)).astype(BF16)
    qbd_ref[...] = jnp.zeros(qbd_ref.shape, BF16)
    for q in range(n_pairs):
        for tl in range(2):
            s0 = (2 * q + tl + 1) * gw
            for g in range(ng):
                qbd_ref[tl * LANES + g * gw:tl * LANES + (g + 1) * gw, g * LANES:(g + 1) * LANES] = (
                    cpow_ref[g, s0:s0 + gw, :].astype(BF16))
        y = lax.dot_general(h_in, qbd_ref[...], _NT, preferred_element_type=F32)
        y = y + lax.dot_general(xall_ref[:, 0:(q + 1) * pw], wrev_ref[:, (n_pairs - 1 - q) * pw:], _NT,
                                preferred_element_type=F32)
        y = jax.nn.gelu(y)
        rs_ref[pl.ds(2 * q, n_chunks, stride=t_len), :] = y[:, 0:LANES]
        rs_ref[pl.ds(2 * q + 1, n_chunks, stride=t_len), :] = y[:, LANES:]
    o_ref[...] = rs_ref[...].astype(o_ref.dtype)


def _ssm(proj2, tables, dvec, layer, bsz, d_ssm):
    cpow, bshift, pin, adec = tables
    rows = proj2.shape[0]
    nc = rows // bsz // SSM_CHUNK
    n_chunks = bsz * nc
    ng = SSM_LANE_GROUPS
    n_pairs = SSM_CHUNK // 2
    wide = ng * LANES
    cp_rows = cpow.shape[2]
    w = SSM_CHUNK * SSM_GROUP
    return pl.pallas_call(
        functools.partial(_ssm_kernel, bsz=bsz, nc=nc),
        grid=(d_ssm // LANES,),
        in_specs=[
            pl.BlockSpec((rows, LANES), lambda j: (0, j), pipeline_mode=pl.Buffered(1)),
            pl.BlockSpec((None, ng, cp_rows, LANES), lambda j: (layer, j, 0, 0)),
            pl.BlockSpec((None, ng, LANES, LANES), lambda j: (layer, j, 0, 0)),
            pl.BlockSpec((None, ng, w, LANES), lambda j: (layer, j, 0, 0)),
            pl.BlockSpec((None, ng, 2, 1, LANES), lambda j: (layer, j, 0, 0, 0)),
            pl.BlockSpec((None, 1, 1, LANES), lambda j: (layer, j, 0, 0)),
        ],
        out_specs=pl.BlockSpec((rows, LANES), lambda j: (0, j)),
        out_shape=jax.ShapeDtypeStruct((rows, d_ssm), BF16),
        scratch_shapes=[
            pltpu.VMEM((rows, LANES), F32),
            pltpu.VMEM((n_chunks, SSM_CHUNK * LANES), BF16),
            pltpu.VMEM((n_chunks, wide), F32),
            pltpu.VMEM((n_chunks, wide), F32),
            pltpu.VMEM((ng, w, LANES), F32),
            pltpu.VMEM((SSM_CHUNK + 1, LANES, LANES), BF16),
            pltpu.VMEM((2 * LANES, n_pairs * 2 * LANES), BF16),
            pltpu.VMEM((SSM_STATE_POS * LANES, wide), BF16),
            pltpu.VMEM((2 * LANES, wide), BF16),
        ],
        compiler_params=_params(1),
        name="s5_chunked",
    )(proj2, cpow, bshift, pin, adec, dvec)


def _merge_kernel(y_ref, att_ref, ga_ref, gb_ref, x_ref, wglu_ref, wus_ref, wua_ref, wout_ref,
                  g_ref, o_ref):
    tm = x_ref.shape[0]
    strip = min(MERGE_STRIP, tm)
    for r0 in range(0, tm, strip):
        rows = slice(r0, r0 + strip)
        y = y_ref[rows, :]
        z = jnp.dot(y, wglu_ref[...], preferred_element_type=F32)
        ya = jnp.dot(att_ref[rows, :], wua_ref[...], preferred_element_type=F32)
        s5 = (y.astype(F32) * jax.nn.sigmoid(z)).astype(BF16)
        ys = jnp.dot(s5, wus_ref[...], preferred_element_type=F32)
        m = (jax.nn.sigmoid(ga_ref[rows, :].astype(F32)) * ys
             + jax.nn.sigmoid(gb_ref[rows, :].astype(F32)) * ya)
        o = jnp.dot(m.astype(BF16), wout_ref[...], preferred_element_type=F32)
        o_ref[rows, :] = x_ref[rows, :] + _rmsnorm_rows(o, g_ref[...])


def _merge(y2, att2, proj2, x2, w_glu, w_us, w_ua, w_out, layer, g_post):
    n_rows, d = x2.shape
    d_ssm = y2.shape[1]
    d_attn = att2.shape[1]
    tm = min(ROW_TILE_MERGE, n_rows)
    ga_blk = (d_ssm + 3 * d_attn) // d

    def weight(wt):
        return pl.BlockSpec((None,) + wt.shape[1:], lambda i: (layer, 0, 0),
                            pipeline_mode=pl.Buffered(1))

    return pl.pallas_call(
        _merge_kernel,
        grid=(n_rows // tm,),
        in_specs=[
            pl.BlockSpec((tm, d_ssm), lambda i: (i, 0)),
            pl.BlockSpec((tm, d_attn), lambda i: (i, 0)),
            pl.BlockSpec((tm, d), lambda i: (i, ga_blk)),
            pl.BlockSpec((tm, d), lambda i: (i, ga_blk + 1)),
            pl.BlockSpec((tm, d), lambda i: (i, 0)),
            weight(w_glu), weight(w_us), weight(w_ua), weight(w_out),
            pl.BlockSpec((1, d), lambda i: (0, 0)),
        ],
        out_specs=pl.BlockSpec((tm, d), lambda i: (i, 0)),
        out_shape=jax.ShapeDtypeStruct((n_rows, d), F32),
        compiler_params=_params(1),
        name="merge",
    )(y2, att2, proj2, proj2, x2, w_glu, w_us, w_ua, w_out, g_post)


def _ffn_kernel(x_ref, gpre_ref, wa_ref, wv_ref, cwa_ref, cwv_ref, cba_ref, cbv_ref, wd_ref,
                gpost_ref, o_ref, h_ref, za_ref, zv_ref, halo_a_ref, halo_v_ref,
                *, tiles_per_seq):
    i = pl.program_id(0)
    c = pl.program_id(1)
    tm = x_ref.shape[0]

    @pl.when(c == 0)
    def _():
        _rmsnorm_into(h_ref, x_ref, gpre_ref)
        o_ref[...] = jnp.zeros_like(o_ref)

    @pl.when((i % tiles_per_seq) == 0)
    def _():
        halo_a_ref[c] = jnp.zeros(halo_a_ref.shape[1:], F32)
        halo_v_ref[c] = jnp.zeros(halo_v_ref.shape[1:], F32)

    za_ref[0:SUBLANES, :] = halo_a_ref[c]
    zv_ref[0:SUBLANES, :] = halo_v_ref[c]
    cwa, cwv = cwa_ref[...], cwv_ref[...]
    cba, cbv = cba_ref[...], cbv_ref[...]

    def up(r0):
        h = h_ref[r0:r0 + FFN_STRIP, :]
        out_rows = slice(SUBLANES + r0, SUBLANES + r0 + FFN_STRIP)
        za_ref[out_rows, :] = jnp.dot(h, wa_ref[...], preferred_element_type=F32)
        zv_ref[out_rows, :] = jnp.dot(h, wv_ref[...], preferred_element_type=F32)

    def conv_strip(z_ref, cw, bias, r0):
        base = SUBLANES + r0
        z0 = z_ref[base:base + FFN_STRIP, :]
        z1 = z_ref[base - 1:base - 1 + FFN_STRIP, :]
        z2 = z_ref[base - 2:base - 2 + FFN_STRIP, :]
        return cw[0:1, :] * z2 + cw[1:2, :] * z1 + cw[2:3, :] * z0 + bias

    def finish(r0):
        a = conv_strip(za_ref, cwa, cba, r0)
        v = conv_strip(zv_ref, cwv, cbv, r0)
        gated = (a * jax.nn.sigmoid(a) * v).astype(BF16)
        o_ref[r0:r0 + FFN_STRIP, :] += jnp.dot(gated, wd_ref[...], preferred_element_type=F32)

    strips = list(range(0, tm, FFN_STRIP))
    up(strips[0])
    for k, r0 in enumerate(strips):
        if k + 1 < len(strips):
            up(strips[k + 1])
        finish(r0)

    halo_a_ref[c] = za_ref[tm:tm + SUBLANES, :]
    halo_v_ref[c] = zv_ref[tm:tm + SUBLANES, :]

    @pl.when(c == pl.num_programs(1) - 1)
    def _():
        step_rows = min(NORM_ROWS, tm)

        def finalize(r, carry):
            rows = pl.ds(pl.multiple_of(r * step_rows, step_rows), step_rows)
            o_ref[rows, :] = x_ref[rows, :] + _rmsnorm_rows(o_ref[rows, :], gpost_ref[...])
            return carry

        lax.fori_loop(0, tm // step_rows, finalize, 0)


def _ffn(x2, g_pre, w_up, conv_w, conv_b, w_down, layer, g_post, seq_len):
    n_rows, d = x2.shape
    d_ff = w_down.shape[1]
    tm = min(ROW_TILE_FFN, seq_len)
    tf = COL_TILE_FFN
    nff = d_ff // tf
    assert tm % FFN_STRIP == 0
    const = lambda i, c: (0, 0)
    zbuf = pltpu.VMEM((SUBLANES + tm, tf), F32)
    return pl.pallas_call(
        functools.partial(_ffn_kernel, tiles_per_seq=seq_len // tm),
        grid=(n_rows // tm, nff),
        in_specs=[
            pl.BlockSpec((tm, d), lambda i, c: (i, 0)),
            pl.BlockSpec((1, d), const),
            pl.BlockSpec((None, d, tf), lambda i, c: (layer, 0, c)),
            pl.BlockSpec((None, d, tf), lambda i, c: (layer, 0, nff + c)),
            pl.BlockSpec((None, CONV_W, tf), lambda i, c: (layer, 0, c)),
            pl.BlockSpec((None, CONV_W, tf), lambda i, c: (layer, 0, nff + c)),
            pl.BlockSpec((None, 1, tf), lambda i, c: (layer, 0, c)),
            pl.BlockSpec((None, 1, tf), lambda i, c: (layer, 0, nff + c)),
            pl.BlockSpec((None, tf, d), lambda i, c: (layer, c, 0)),
            pl.BlockSpec((1, d), const),
        ],
        out_specs=pl.BlockSpec((tm, d), lambda i, c: (i, 0)),
        out_shape=jax.ShapeDtypeStruct((n_rows, d), F32),
        scratch_shapes=[
            pltpu.VMEM((tm, d), BF16),
            zbuf, zbuf,
            pltpu.VMEM((nff, SUBLANES, tf), F32),
            pltpu.VMEM((nff, SUBLANES, tf), F32),
        ],
        compiler_params=_params(2),
        name="convglu_ffn",
    )(x2, g_pre, w_up, w_up, conv_w, conv_w, conv_b, conv_b, w_down, g_post)


def kernel(x, g_pre_mix, w_in, lam_re, lam_im, log_dt, b_re, b_im, c_re, c_im, d_skip, w_glu,
           w_up_ssm, w_up_attn, w_out, g_post_mix, g_pre_ffn, w_ffn_up, conv_w, conv_b,
           w_ffn_down, g_post_ffn):
    bsz, seq_len, d = x.shape
    depth = w_in.shape[0]
    d_ssm = w_glu.shape[1]
    d_attn = w_up_attn.shape[1]
    n_heads = d_attn // HEAD_DIM
    n_groups = d_ssm // SSM_GROUP
    assert seq_len % (ATTN_Q_BLOCKS * MOBA_BLOCK) == 0 and seq_len % SSM_CHUNK == 0
    assert d_ssm % d_attn == 0
    nb = seq_len // MOBA_BLOCK
    nc = seq_len // SSM_CHUNK
    n_rows = bsz * seq_len
    row = lambda v: v.reshape(1, -1)

    w_in_bf = w_in.astype(BF16)
    w_glu_bf = w_glu.astype(BF16)
    w_us_bf = w_up_ssm.astype(BF16)
    w_ua_bf = w_up_attn.astype(BF16)
    w_out_bf = w_out.astype(BF16)
    w_fu_bf = w_ffn_up.astype(BF16)
    w_fd_bf = w_ffn_down.astype(BF16)
    conv_b3 = conv_b[:, None, :]
    tables = jax.vmap(_ssm_tables)(lam_re, lam_im, log_dt, b_re, b_im, c_re, c_im, d_skip)
    dvec = d_skip.reshape(depth, d_ssm // LANES, 1, LANES)
    col_scale = jnp.ones((w_in.shape[2],), F32).at[d_ssm:d_ssm + d_attn].set(1.0 / math.sqrt(HEAD_DIM))

    x2 = x.reshape(n_rows, d)
    for l in range(depth):
        proj2, hbar = _inproj(x2, row(g_pre_mix[l]), w_in_bf, l, row(col_scale))

        hbar = hbar.reshape(bsz, nb, 1, d)
        hbar_t = jnp.broadcast_to(hbar, (bsz, nb, n_heads, d)).reshape(bsz * nb * n_heads, d)
        kbd = _kmean(hbar_t, w_in, l, nb, d_ssm, d_attn).reshape(bsz, n_heads * nb, d_attn)
        z = _gatevec(w_in, l, kbd, d_ssm)
        bias = _select(x2.reshape(bsz, seq_len, d), row(g_pre_mix[l]), z, nb)
        att = _attention(proj2.reshape(bsz, seq_len, -1), bias, nb, d_ssm)

        y2 = _ssm(proj2, tables, dvec, l, bsz, d_ssm)

        x2 = _merge(y2, att.reshape(n_rows, d_attn), proj2, x2, w_glu_bf, w_us_bf, w_ua_bf,
                    w_out_bf, l, row(g_post_mix[l]))
        x2 = _ffn(x2, row(g_pre_ffn[l]), w_fu_bf, conv_w, conv_b3, w_fd_bf, l,
                  row(g_post_ffn[l]), seq_len)
    return x2.reshape(bsz, seq_len, d)
```

```python
import functools
import math

import jax
import jax.numpy as jnp
from jax import lax
from jax.experimental import pallas as pl
from jax.experimental.pallas import tpu as pltpu

F32 = jnp.float32
BF16 = jnp.bfloat16

EPS = 1e-6
NEG = -1e30
SSM_GROUP = 16
HEAD_DIM = 128
MOBA_BLOCK = 256
MOBA_TOPK = 3
CONV_W = 3

LANES = 128
SUBLANES = 8
VMEM_LIMIT_BYTES = 56 * 1024 * 1024

SSM_CHUNK = 32
SSM_LANE_GROUPS = LANES // SSM_GROUP
SSM_STATE_POS = 8
ROW_TILE_PROJ = 1024
COL_TILE_PROJ = 2048
ROW_TILE_GATE = 512
ROW_TILE_MERGE = 512
MERGE_STRIP = 256
ROW_TILE_FFN = 512
COL_TILE_FFN = 512
FFN_STRIP = 256
K_TILE_GATE = 512
NORM_ROWS = 128
ATTN_Q_BLOCKS = 2
ATTN_HEADS = 2

_NT = (((1,), (1,)), ((), ()))
_HI = lax.Precision.HIGHEST


def _params(n_axes):
    return pltpu.CompilerParams(
        dimension_semantics=("arbitrary",) * n_axes,
        vmem_limit_bytes=VMEM_LIMIT_BYTES,
    )


def _rmsnorm_rows(x, g):
    ms = jnp.mean(x * x, axis=-1, keepdims=True)
    return x * lax.rsqrt(ms + EPS) * g


def _rmsnorm_into(h_ref, x_ref, g_ref):
    step_rows = min(NORM_ROWS, x_ref.shape[0])

    def step(r, carry):
        r0 = pl.multiple_of(r * step_rows, step_rows)
        h_ref[pl.ds(r0, step_rows), :] = _rmsnorm_rows(
            x_ref[pl.ds(r0, step_rows), :], g_ref[...]).astype(h_ref.dtype)
        return carry

    lax.fori_loop(0, x_ref.shape[0] // step_rows, step, 0)


def _inproj_kernel(x_ref, g_ref, w_ref, cs_ref, o_ref, hb_ref, h_ref):
    @pl.when(pl.program_id(1) == 0)
    def _():
        def step(r, carry):
            r0 = pl.multiple_of(r * MOBA_BLOCK, MOBA_BLOCK)
            hn = _rmsnorm_rows(x_ref[pl.ds(r0, MOBA_BLOCK), :], g_ref[...])
            h_ref[pl.ds(r0, MOBA_BLOCK), :] = hn.astype(h_ref.dtype)
            hb_ref[r] = jnp.mean(hn, axis=0, keepdims=True)
            return carry

        lax.fori_loop(0, x_ref.shape[0] // MOBA_BLOCK, step, 0)

    acc = jnp.dot(h_ref[...], w_ref[...], preferred_element_type=F32)
    o_ref[...] = (acc * cs_ref[...]).astype(o_ref.dtype)


def _inproj(x2, g, w_bf, layer, col_scale):
    n_rows, d = x2.shape
    d_in = w_bf.shape[2]
    tm = min(ROW_TILE_PROJ, n_rows)
    tn = COL_TILE_PROJ
    assert tm % MOBA_BLOCK == 0
    blocks_per_tile = tm // MOBA_BLOCK
    return pl.pallas_call(
        _inproj_kernel,
        grid=(n_rows // tm, d_in // tn),
        in_specs=[
            pl.BlockSpec((tm, d), lambda i, j: (i, 0)),
            pl.BlockSpec((1, d), lambda i, j: (0, 0)),
            pl.BlockSpec((None, d, tn), lambda i, j: (layer, 0, j)),
            pl.BlockSpec((1, tn), lambda i, j: (0, j)),
        ],
        out_specs=[
            pl.BlockSpec((tm, tn), lambda i, j: (i, j)),
            pl.BlockSpec((blocks_per_tile, 1, d), lambda i, j: (i, 0, 0)),
        ],
        out_shape=[
            jax.ShapeDtypeStruct((n_rows, d_in), BF16),
            jax.ShapeDtypeStruct((n_rows // MOBA_BLOCK, 1, d), F32),
        ],
        scratch_shapes=[pltpu.VMEM((tm, d), BF16)],
        compiler_params=_params(2),
        name="inproj",
    )(x2, g, w_bf, col_scale)


def _kmean_kernel(hb_ref, wk_ref, o_ref, *, nb):
    @pl.when(pl.program_id(0) == 0)
    def _():
        o_ref[...] = jnp.zeros_like(o_ref)

    o_ref[...] += jnp.dot(hb_ref[...], wk_ref[...], preferred_element_type=F32, precision=_HI)

    @pl.when(pl.program_id(0) == pl.num_programs(0) - 1)
    def _():
        r = lax.broadcasted_iota(jnp.int32, o_ref.shape, 0)
        c = lax.broadcasted_iota(jnp.int32, o_ref.shape, 1)
        n_heads = o_ref.shape[1] // HEAD_DIM
        keep = (c // HEAD_DIM) == (r % n_heads)
        o_ref[...] = jnp.where(keep, o_ref[...], 0.0)


def _kmean(hbar_t, w_in, layer, nb, d_ssm, d_attn):
    rows, d = hbar_t.shape
    k_col_block = (d_ssm + d_attn) // d_attn
    tk = K_TILE_GATE
    return pl.pallas_call(
        functools.partial(_kmean_kernel, nb=nb),
        grid=(d // tk,),
        in_specs=[
            pl.BlockSpec((rows, tk), lambda kk: (0, kk)),
            pl.BlockSpec((None, tk, d_attn), lambda kk: (layer, kk, k_col_block)),
        ],
        out_specs=pl.BlockSpec((rows, d_attn), lambda kk: (0, 0)),
        out_shape=jax.ShapeDtypeStruct((rows, d_attn), F32),
        compiler_params=_params(1),
        name="kmean",
    )(hbar_t, w_in)


def _gatevec_kernel(wq_ref, kbd_ref, o_ref):
    o_ref[0] = lax.dot_general(wq_ref[...], kbd_ref[0], _NT, preferred_element_type=F32,
                               precision=_HI)


def _gatevec(w_in, layer, kbd, d_ssm):
    bsz, n_gate, d_attn = kbd.shape
    d = w_in.shape[1]
    q_col_block = d_ssm // d_attn
    tk = K_TILE_GATE
    return pl.pallas_call(
        _gatevec_kernel,
        grid=(bsz, d // tk),
        in_specs=[
            pl.BlockSpec((None, tk, d_attn), lambda b, kk: (layer, kk, q_col_block)),
            pl.BlockSpec((1, n_gate, d_attn), lambda b, kk: (b, 0, 0)),
        ],
        out_specs=pl.BlockSpec((1, tk, n_gate), lambda b, kk: (b, kk, 0)),
        out_shape=jax.ShapeDtypeStruct((bsz, d, n_gate), F32),
        compiler_params=_params(2),
        name="gatevec",
    )(w_in, kbd)


def _select_kernel(x_ref, g_ref, z_ref, o_ref, *, nb, tq):
    h = _rmsnorm_rows(x_ref[0], g_ref[...])
    gate = jnp.dot(h, z_ref[0], preferred_element_type=F32, precision=_HI)
    n_heads = gate.shape[1] // nb
    row = lax.broadcasted_iota(jnp.int32, gate.shape, 0) + pl.program_id(1) * tq
    qblk = row // MOBA_BLOCK
    j = lax.broadcasted_iota(jnp.int32, gate.shape, 1) // n_heads
    past = j < qblk
    gate = jnp.where(past, gate, NEG)
    cnt = jnp.zeros(gate.shape, F32)
    for r in range(1, nb):
        other = pltpu.roll(gate, r * n_heads, 1)
        wins_tie = jnp.where(other >= gate, 1.0, 0.0)
        wins_strict = jnp.where(other > gate, 1.0, 0.0)
        cnt = cnt + jnp.where(j >= r, wins_tie, wins_strict)
    keep = jnp.logical_or(jnp.logical_and(past, cnt < float(MOBA_TOPK)), j == qblk)
    o_ref[0] = jnp.where(keep, 0.0, NEG)


def _select(x3, g, z, nb):
    bsz, s, d = x3.shape
    n_gate = z.shape[2]
    tq = min(ROW_TILE_GATE, s)
    return pl.pallas_call(
        functools.partial(_select_kernel, nb=nb, tq=tq),
        grid=(bsz, s // tq),
        in_specs=[
            pl.BlockSpec((1, tq, d), lambda b, i: (b, i, 0)),
            pl.BlockSpec((1, d), lambda b, i: (0, 0)),
            pl.BlockSpec((1, d, n_gate), lambda b, i: (b, 0, 0)),
        ],
        out_specs=pl.BlockSpec((1, tq, n_gate), lambda b, i: (b, i, 0)),
        out_shape=jax.ShapeDtypeStruct((bsz, s, n_gate), F32),
        compiler_params=_params(2),
        name="select",
    )(x3, g, z)


def _attn_kernel(q_ref, k_ref, v_ref, bias_ref, o_ref, qa_ref, s_ref, m_ref, acc_ref, *, nb):
    hp = pl.program_id(1)
    i = pl.program_id(2)
    blk = MOBA_BLOCK
    tq = q_ref.shape[1]
    n_gate = bias_ref.shape[2]
    heads = range(ATTN_HEADS)
    mask_rows = bias_ref[0].astype(BF16)
    for hh in heads:
        qa_ref[hh, :, :HEAD_DIM] = q_ref[0, :, hh * HEAD_DIM:(hh + 1) * HEAD_DIM]
        qa_ref[hh, :, HEAD_DIM:] = mask_rows
    m_ref[...] = jnp.full(m_ref.shape, -jnp.inf, F32)
    acc_ref[...] = jnp.zeros(acc_ref.shape, F32)
    lane = lax.broadcasted_iota(jnp.int32, (blk, n_gate), 1)
    ones = jnp.ones((blk, HEAD_DIM), BF16)

    def scores(slot, hh, jb):
        st = pl.multiple_of(jb * blk, blk)
        col = jb * (n_gate // nb) + hp * ATTN_HEADS + hh
        onehot = jnp.where(lane == col, 1.0, 0.0).astype(BF16)
        kj = k_ref[0, pl.ds(st, blk), hh * HEAD_DIM:(hh + 1) * HEAD_DIM]
        s_ref[slot, hh] = lax.dot_general(qa_ref[hh], jnp.concatenate([kj, onehot], axis=1), _NT,
                                          preferred_element_type=F32)

    def update(slot, hh, jb, causal):
        s = s_ref[slot, hh]
        if causal:
            qpos = lax.broadcasted_iota(jnp.int32, s.shape, 0) + i * tq
            kpos = lax.broadcasted_iota(jnp.int32, s.shape, 1) + jb * blk
            s = jnp.where(kpos <= qpos, s, NEG)
        st = pl.multiple_of(jb * blk, blk)
        m_old = m_ref[hh]
        m_new = jnp.maximum(m_old, jnp.max(s, axis=-1, keepdims=True))
        alpha = jnp.exp(m_old - m_new)
        p = jnp.exp(s - jnp.concatenate([m_new] * (blk // LANES), axis=1))
        vj = v_ref[0, pl.ds(st, blk), hh * HEAD_DIM:(hh + 1) * HEAD_DIM]
        acc_ref[hh] = jnp.concatenate([alpha, alpha], axis=1) * acc_ref[hh] + jnp.dot(
            p.astype(BF16), jnp.concatenate([vj, ones], axis=1), preferred_element_type=F32)
        m_ref[hh] = m_new

    for hh in heads:
        scores(0, hh, 0)

    def pair(jj, carry):
        for hh in heads:
            scores(1, hh, 2 * jj + 1)
            update(0, hh, 2 * jj, False)
            scores(0, hh, 2 * jj + 2)
            update(1, hh, 2 * jj + 1, False)
        return carry

    lax.fori_loop(0, i, pair, 0)
    for hh in heads:
        scores(1, hh, 2 * i + 1)
        update(0, hh, 2 * i, True)
        update(1, hh, 2 * i + 1, True)
        o_ref[0, :, hh * HEAD_DIM:(hh + 1) * HEAD_DIM] = (
            acc_ref[hh, :, :HEAD_DIM] / acc_ref[hh, :, HEAD_DIM:]).astype(o_ref.dtype)


def _attention(proj3, bias, nb, d_ssm):
    bsz, s, _ = proj3.shape
    n_gate = bias.shape[2]
    n_heads = n_gate // nb
    d_attn = n_heads * HEAD_DIM
    q0 = d_ssm // HEAD_DIM
    k0 = q0 + n_heads
    v0 = k0 + n_heads
    assert ATTN_Q_BLOCKS == 2 and nb % ATTN_Q_BLOCKS == 0
    assert n_heads % ATTN_HEADS == 0 and q0 % ATTN_HEADS == 0
    tq = ATTN_Q_BLOCKS * MOBA_BLOCK
    hw = ATTN_HEADS * HEAD_DIM
    return pl.pallas_call(
        functools.partial(_attn_kernel, nb=nb),
        grid=(bsz, n_heads // ATTN_HEADS, s // tq),
        in_specs=[
            pl.BlockSpec((1, tq, hw), lambda b, h, i: (b, i, q0 // ATTN_HEADS + h)),
            pl.BlockSpec((1, s, hw), lambda b, h, i: (b, 0, k0 // ATTN_HEADS + h)),
            pl.BlockSpec((1, s, hw), lambda b, h, i: (b, 0, v0 // ATTN_HEADS + h)),
            pl.BlockSpec((1, tq, n_gate), lambda b, h, i: (b, i, 0)),
        ],
        out_specs=pl.BlockSpec((1, tq, hw), lambda b, h, i: (b, i, h)),
        out_shape=jax.ShapeDtypeStruct((bsz, s, d_attn), BF16),
        scratch_shapes=[
            pltpu.VMEM((ATTN_HEADS, tq, HEAD_DIM + n_gate), BF16),
            pltpu.VMEM((2, ATTN_HEADS, tq, MOBA_BLOCK), F32),
            pltpu.VMEM((ATTN_HEADS, tq, LANES), F32),
            pltpu.VMEM((ATTN_HEADS, tq, 2 * HEAD_DIM), F32),
        ],
        compiler_params=_params(3),
        name="moba_attn",
    )(proj3, proj3, proj3, bias)


def _ssm_tables(lam_re, lam_im, log_dt, b_re, b_im, c_re, c_im, d_skip):
    t_len = SSM_CHUNK
    n_groups, n_state = lam_re.shape
    n_ch = SSM_GROUP
    assert 2 * n_state == LANES
    w = t_len * n_ch
    dt = jnp.exp(log_dt)[:, None]
    ar = lam_re * dt
    ai = lam_im * dt
    steps = jnp.arange(t_len + 1, dtype=F32)[None, :, None]
    mag = jnp.exp(ar[:, None, :] * steps)
    pw_re = mag * jnp.cos(ai[:, None, :] * steps)
    pw_im = mag * jnp.sin(ai[:, None, :] * steps)
    e1 = jnp.expm1(ar)
    sh = jnp.sin(0.5 * ai)
    num_re = e1 * jnp.cos(ai) - 2.0 * sh * sh
    num_im = (e1 + 1.0) * jnp.sin(ai)
    den = lam_re * lam_re + lam_im * lam_im
    coef_re = (num_re * lam_re + num_im * lam_im) / den
    coef_im = (num_im * lam_re - num_re * lam_im) / den
    bb_re = coef_re[..., None] * b_re - coef_im[..., None] * b_im
    bb_im = coef_re[..., None] * b_im + coef_im[..., None] * b_re
    cp_re = c_re[:, None] * pw_re[:, :, None, :] - c_im[:, None] * pw_im[:, :, None, :]
    cp_im = c_re[:, None] * pw_im[:, :, None, :] + c_im[:, None] * pw_re[:, :, None, :]
    cpow = jnp.concatenate([cp_re, -cp_im], axis=-1).reshape(n_groups, (t_len + 1) * n_ch, LANES)
    lane_of = (jnp.arange(n_groups) % SSM_LANE_GROUPS)[:, None] * n_ch + jnp.arange(n_ch)[None, :]
    place = (lane_of[:, :, None] == jnp.arange(LANES)[None, None, :]).astype(F32)
    bshift = jnp.einsum('gpm,gml->gpl', jnp.concatenate([bb_re, bb_im], axis=1), place, precision=_HI)
    rev_re = pw_re[:, :t_len][:, ::-1][:, :, None, :]
    rev_im = pw_im[:, :t_len][:, ::-1][:, :, None, :]
    bt_re = bb_re.transpose(0, 2, 1)[:, None]
    bt_im = bb_im.transpose(0, 2, 1)[:, None]
    pin = jnp.concatenate([rev_re * bt_re - rev_im * bt_im, rev_re * bt_im + rev_im * bt_re], axis=-1)
    pad_p = LANES - n_state
    pin = pin.reshape(n_groups, w, LANES)
    adec = jnp.stack([pw_re[:, t_len], pw_im[:, t_len]], axis=1)[:, :, None, :]
    adec = jnp.pad(adec, ((0, 0), (0, 0), (0, 0), (0, pad_p)))
    return cpow, bshift, pin.astype(BF16), adec


def _ssm_kernel(x_ref, cpow_ref, bshift_ref, pin_ref, adec_ref, dvec_ref, o_ref,
                rs_ref, xall_ref, hre_ref, him_ref, pre_ref, pim_ref, kk_ref, wt_ref, wrev_ref,
                pbd_ref, qbd_ref, *, bsz, nc):
    t_len = SSM_CHUNK
    n_pairs = t_len // 2
    ng = SSM_LANE_GROUPS
    gw = SSM_GROUP
    n_state = LANES // 2
    n_chunks = bsz * nc
    wide = ng * LANES

    for g in range(ng):
        kk_ref[g] = jnp.dot(cpow_ref[g, 0:t_len * gw, :], bshift_ref[g], preferred_element_type=F32,
                            precision=_HI)
    row = lax.broadcasted_iota(jnp.int32, (LANES, LANES), 0)
    col = lax.broadcasted_iota(jnp.int32, (LANES, LANES), 1)
    wt_ref[0] = jnp.zeros((LANES, LANES), BF16)
    for l in range(t_len):
        tile = kk_ref[:, l * gw:(l + 1) * gw, :].reshape(LANES, LANES)
        if l == 0:
            tile = tile + jnp.where(row == col, dvec_ref[0], 0.0)
        wt_ref[l + 1] = tile.astype(BF16)
    pw = 2 * LANES
    for d in range(n_pairs):
        c0 = (n_pairs - 1 - d) * pw
        wrev_ref[0:LANES, c0:c0 + LANES] = wt_ref[2 * d + 1]
        wrev_ref[0:LANES, c0 + LANES:c0 + pw] = wt_ref[2 * d]
        wrev_ref[LANES:, c0:c0 + LANES] = wt_ref[2 * d + 2]
        wrev_ref[LANES:, c0 + LANES:c0 + pw] = wt_ref[2 * d + 1]

    rs_ref[...] = x_ref[...].astype(F32)
    for s in range(t_len):
        xall_ref[:, s * LANES:(s + 1) * LANES] = rs_ref[pl.ds(s, n_chunks, stride=t_len), :].astype(BF16)

    pbd_ref[...] = jnp.zeros(pbd_ref.shape, BF16)
    for k in range(t_len // SSM_STATE_POS):
        for sl in range(SSM_STATE_POS):
            s0 = (k * SSM_STATE_POS + sl) * gw
            for g in range(ng):
                rows = slice(sl * LANES + g * gw, sl * LANES + (g + 1) * gw)
                lanes = slice(g * LANES, (g + 1) * LANES)
                pbd_ref[rows, lanes] = pin_ref[g, s0:s0 + gw, :]
        xk = xall_ref[:, k * SSM_STATE_POS * LANES:(k + 1) * SSM_STATE_POS * LANES]
        h_cat = jnp.dot(xk, pbd_ref[...], preferred_element_type=F32)
        if k == 0:
            hre_ref[...] = h_cat
        else:
            hre_ref[...] += h_cat
    lower = (lax.broadcasted_iota(jnp.int32, (n_chunks, wide), 1) % LANES) < n_state
    h_cat = hre_ref[...]
    him_ref[...] = jnp.where(lower, pltpu.roll(h_cat, wide - n_state, 1), 0.0)
    hre_ref[...] = jnp.where(lower, h_cat, 0.0)

    a_re = jnp.concatenate([adec_ref[g, 0] for g in range(ng)], axis=1)
    a_im = jnp.concatenate([adec_ref[g, 1] for g in range(ng)], axis=1)

    pre_ref[...] = jnp.zeros(pre_ref.shape, F32)
    pim_ref[...] = jnp.zeros(pim_ref.shape, F32)
    for g in range(ng):
        lanes = slice(g * LANES, (g + 1) * LANES)
        for b in range(bsz):
            pre_ref[g, pl.ds(b, nc, stride=SUBLANES), :] = hre_ref[b * nc:(b + 1) * nc, lanes]
            pim_ref[g, pl.ds(b, nc, stride=SUBLANES), :] = him_ref[b * nc:(b + 1) * nc, lanes]
    a_re3 = jnp.stack([adec_ref[g, 0] for g in range(ng)], axis=0)
    a_im3 = jnp.stack([adec_ref[g, 1] for g in range(ng)], axis=0)

    def step(c, carry):
        s_re, s_im = carry
        rows = pl.ds(pl.multiple_of(c * SUBLANES, SUBLANES), SUBLANES)
        loc_re = pre_ref[:, rows, :]
        loc_im = pim_ref[:, rows, :]
        pre_ref[:, rows, :] = s_re
        pim_ref[:, rows, :] = s_im
        return (a_re3 * s_re - a_im3 * s_im + loc_re, a_re3 * s_im + a_im3 * s_re + loc_im)

    zero = jnp.zeros((ng, SUBLANES, LANES), F32)
    lax.fori_loop(0, nc, step, (zero, zero), unroll=2)
    for g in range(ng):
        lanes = slice(g * LANES, (g + 1) * LANES)
        for b in range(bsz):
            hre_ref[b * nc:(b + 1) * nc, lanes] = pre_ref[g, pl.ds(b, nc, stride=SUBLANES), :]
            him_ref[b * nc:(b + 1) * nc, lanes] = pim_ref[g, pl.ds(b, nc, stride=SUBLANES), :]

    h_in = (hre_ref[...] + pltpu.roll(him_ref[...], n_state, 1)).astype(BF16)
    qbd_ref[...] = jnp.zeros(qbd_ref.shape, BF16)
    for q in range(n_pairs):
        for tl in range(2):
            s0 = (2 * q + tl + 1) * gw
            for g in range(ng):
                qbd_ref[tl * LANES + g * gw:tl * LANES + (g + 1) * gw, g * LANES:(g + 1) * LANES] = (
                    cpow_ref[g, s0:s0 + gw, :].astype(BF16))
        y = lax.dot_general(h_in, qbd_ref[...], _NT, preferred_element_type=F32)
        y = y + lax.dot_general(xall_ref[:, 0:(q + 1) * pw], wrev_ref[:, (n_pairs - 1 - q) * pw:], _NT,
                                preferred_element_type=F32)
        y = jax.nn.gelu(y)
        rs_ref[pl.ds(2 * q, n_chunks, stride=t_len), :] = y[:, 0:LANES]
        rs_ref[pl.ds(2 * q + 1, n_chunks, stride=t_len), :] = y[:, LANES:]
    o_ref[...] = rs_ref[...].astype(o_ref.dtype)


def _ssm(proj2, tables, dvec, layer, bsz, d_ssm):
    cpow, bshift, pin, adec = tables
    rows = proj2.shape[0]
    nc = rows // bsz // SSM_CHUNK
    n_chunks = bsz * nc
    ng = SSM_LANE_GROUPS
    n_pairs = SSM_CHUNK // 2
    wide = ng * LANES
    cp_rows = cpow.shape[2]
    w = SSM_CHUNK * SSM_GROUP
    return pl.pallas_call(
        functools.partial(_ssm_kernel, bsz=bsz, nc=nc),
        grid=(d_ssm // LANES,),
        in_specs=[
            pl.BlockSpec((rows, LANES), lambda j: (0, j), pipeline_mode=pl.Buffered(1)),
            pl.BlockSpec((None, ng, cp_rows, LANES), lambda j: (layer, j, 0, 0)),
            pl.BlockSpec((None, ng, LANES, LANES), lambda j: (layer, j, 0, 0)),
            pl.BlockSpec((None, ng, w, LANES), lambda j: (layer, j, 0, 0)),
            pl.BlockSpec((None, ng, 2, 1, LANES), lambda j: (layer, j, 0, 0, 0)),
            pl.BlockSpec((None, 1, 1, LANES), lambda j: (layer, j, 0, 0)),
        ],
        out_specs=pl.BlockSpec((rows, LANES), lambda j: (0, j)),
        out_shape=jax.ShapeDtypeStruct((rows, d_ssm), BF16),
        scratch_shapes=[
            pltpu.VMEM((rows, LANES), F32),
            pltpu.VMEM((n_chunks, SSM_CHUNK * LANES), BF16),
            pltpu.VMEM((n_chunks, wide), F32),
            pltpu.VMEM((n_chunks, wide), F32),
            pltpu.VMEM((ng, nc * SUBLANES, LANES), F32),
            pltpu.VMEM((ng, nc * SUBLANES, LANES), F32),
            pltpu.VMEM((ng, w, LANES), F32),
            pltpu.VMEM((SSM_CHUNK + 1, LANES, LANES), BF16),
            pltpu.VMEM((2 * LANES, n_pairs * 2 * LANES), BF16),
            pltpu.VMEM((SSM_STATE_POS * LANES, wide), BF16),
            pltpu.VMEM((2 * LANES, wide), BF16),
        ],
        compiler_params=_params(1),
        name="s5_chunked",
    )(proj2, cpow, bshift, pin, adec, dvec)


def _merge_kernel(y_ref, att_ref, ga_ref, gb_ref, x_ref, wglu_ref, wus_ref, wua_ref, wout_ref,
                  g_ref, o_ref):
    tm = x_ref.shape[0]
    strip = min(MERGE_STRIP, tm)
    for r0 in range(0, tm, strip):
        rows = slice(r0, r0 + strip)
        y = y_ref[rows, :]
        z = jnp.dot(y, wglu_ref[...], preferred_element_type=F32)
        ya = jnp.dot(att_ref[rows, :], wua_ref[...], preferred_element_type=F32)
        s5 = (y.astype(F32) * jax.nn.sigmoid(z)).astype(BF16)
        ys = jnp.dot(s5, wus_ref[...], preferred_element_type=F32)
        m = (jax.nn.sigmoid(ga_ref[rows, :].astype(F32)) * ys
             + jax.nn.sigmoid(gb_ref[rows, :].astype(F32)) * ya)
        o = jnp.dot(m.astype(BF16), wout_ref[...], preferred_element_type=F32)
        o_ref[rows, :] = x_ref[rows, :] + _rmsnorm_rows(o, g_ref[...])


def _merge(y2, att2, proj2, x2, w_glu, w_us, w_ua, w_out, layer, g_post):
    n_rows, d = x2.shape
    d_ssm = y2.shape[1]
    d_attn = att2.shape[1]
    tm = min(ROW_TILE_MERGE, n_rows)
    ga_blk = (d_ssm + 3 * d_attn) // d

    def weight(wt):
        return pl.BlockSpec((None,) + wt.shape[1:], lambda i: (layer, 0, 0),
                            pipeline_mode=pl.Buffered(1))

    return pl.pallas_call(
        _merge_kernel,
        grid=(n_rows // tm,),
        in_specs=[
            pl.BlockSpec((tm, d_ssm), lambda i: (i, 0)),
            pl.BlockSpec((tm, d_attn), lambda i: (i, 0)),
            pl.BlockSpec((tm, d), lambda i: (i, ga_blk)),
            pl.BlockSpec((tm, d), lambda i: (i, ga_blk + 1)),
            pl.BlockSpec((tm, d), lambda i: (i, 0)),
            weight(w_glu), weight(w_us), weight(w_ua), weight(w_out),
            pl.BlockSpec((1, d), lambda i: (0, 0)),
        ],
        out_specs=pl.BlockSpec((tm, d), lambda i: (i, 0)),
        out_shape=jax.ShapeDtypeStruct((n_rows, d), F32),
        compiler_params=_params(1),
        name="merge",
    )(y2, att2, proj2, proj2, x2, w_glu, w_us, w_ua, w_out, g_post)


def _ffn_kernel(x_ref, gpre_ref, wa_ref, wv_ref, cwa_ref, cwv_ref, cba_ref, cbv_ref, wd_ref,
                gpost_ref, o_ref, h_ref, za_ref, zv_ref, halo_a_ref, halo_v_ref,
                *, tiles_per_seq):
    i = pl.program_id(0)
    c = pl.program_id(1)
    tm = x_ref.shape[0]

    @pl.when(c == 0)
    def _():
        _rmsnorm_into(h_ref, x_ref, gpre_ref)
        o_ref[...] = jnp.zeros_like(o_ref)

    @pl.when((i % tiles_per_seq) == 0)
    def _():
        halo_a_ref[c] = jnp.zeros(halo_a_ref.shape[1:], F32)
        halo_v_ref[c] = jnp.zeros(halo_v_ref.shape[1:], F32)

    za_ref[0:SUBLANES, :] = halo_a_ref[c]
    zv_ref[0:SUBLANES, :] = halo_v_ref[c]
    cwa, cwv = cwa_ref[...], cwv_ref[...]
    cba, cbv = cba_ref[...], cbv_ref[...]

    def up(r0):
        h = h_ref[r0:r0 + FFN_STRIP, :]
        out_rows = slice(SUBLANES + r0, SUBLANES + r0 + FFN_STRIP)
        za_ref[out_rows, :] = jnp.dot(h, wa_ref[...], preferred_element_type=F32)
        zv_ref[out_rows, :] = jnp.dot(h, wv_ref[...], preferred_element_type=F32)

    def conv_strip(z_ref, cw, bias, r0):
        base = SUBLANES + r0
        z0 = z_ref[base:base + FFN_STRIP, :]
        z1 = z_ref[base - 1:base - 1 + FFN_STRIP, :]
        z2 = z_ref[base - 2:base - 2 + FFN_STRIP, :]
        return cw[0:1, :] * z2 + cw[1:2, :] * z1 + cw[2:3, :] * z0 + bias

    def finish(r0):
        a = conv_strip(za_ref, cwa, cba, r0)
        v = conv_strip(zv_ref, cwv, cbv, r0)
        gated = (a * jax.nn.sigmoid(a) * v).astype(BF16)
        o_ref[r0:r0 + FFN_STRIP, :] += jnp.dot(gated, wd_ref[...], preferred_element_type=F32)

    strips = list(range(0, tm, FFN_STRIP))
    up(strips[0])
    for k, r0 in enumerate(strips):
        if k + 1 < len(strips):
            up(strips[k + 1])
        finish(r0)

    halo_a_ref[c] = za_ref[tm:tm + SUBLANES, :]
    halo_v_ref[c] = zv_ref[tm:tm + SUBLANES, :]

    @pl.when(c == pl.num_programs(1) - 1)
    def _():
        step_rows = min(NORM_ROWS, tm)

        def finalize(r, carry):
            rows = pl.ds(pl.multiple_of(r * step_rows, step_rows), step_rows)
            o_ref[rows, :] = x_ref[rows, :] + _rmsnorm_rows(o_ref[rows, :], gpost_ref[...])
            return carry

        lax.fori_loop(0, tm // step_rows, finalize, 0)


def _ffn(x2, g_pre, w_up, conv_w, conv_b, w_down, layer, g_post, seq_len):
    n_rows, d = x2.shape
    d_ff = w_down.shape[1]
    tm = min(ROW_TILE_FFN, seq_len)
    tf = COL_TILE_FFN
    nff = d_ff // tf
    assert tm % FFN_STRIP == 0
    const = lambda i, c: (0, 0)
    zbuf = pltpu.VMEM((SUBLANES + tm, tf), F32)
    return pl.pallas_call(
        functools.partial(_ffn_kernel, tiles_per_seq=seq_len // tm),
        grid=(n_rows // tm, nff),
        in_specs=[
            pl.BlockSpec((tm, d), lambda i, c: (i, 0)),
            pl.BlockSpec((1, d), const),
            pl.BlockSpec((None, d, tf), lambda i, c: (layer, 0, c)),
            pl.BlockSpec((None, d, tf), lambda i, c: (layer, 0, nff + c)),
            pl.BlockSpec((None, CONV_W, tf), lambda i, c: (layer, 0, c)),
            pl.BlockSpec((None, CONV_W, tf), lambda i, c: (layer, 0, nff + c)),
            pl.BlockSpec((None, 1, tf), lambda i, c: (layer, 0, c)),
            pl.BlockSpec((None, 1, tf), lambda i, c: (layer, 0, nff + c)),
            pl.BlockSpec((None, tf, d), lambda i, c: (layer, c, 0)),
            pl.BlockSpec((1, d), const),
        ],
        out_specs=pl.BlockSpec((tm, d), lambda i, c: (i, 0)),
        out_shape=jax.ShapeDtypeStruct((n_rows, d), F32),
        scratch_shapes=[
            pltpu.VMEM((tm, d), BF16),
            zbuf, zbuf,
            pltpu.VMEM((nff, SUBLANES, tf), F32),
            pltpu.VMEM((nff, SUBLANES, tf), F32),
        ],
        compiler_params=_params(2),
        name="convglu_ffn",
    )(x2, g_pre, w_up, w_up, conv_w, conv_w, conv_b, conv_b, w_down, g_post)


def kernel(x, g_pre_mix, w_in, lam_re, lam_im, log_dt, b_re, b_im, c_re, c_im, d_skip, w_glu,
           w_up_ssm, w_up_attn, w_out, g_post_mix, g_pre_ffn, w_ffn_up, conv_w, conv_b,
           w_ffn_down, g_post_ffn):
    bsz, seq_len, d = x.shape
    depth = w_in.shape[0]
    d_ssm = w_glu.shape[1]
    d_attn = w_up_attn.shape[1]
    n_heads = d_attn // HEAD_DIM
    n_groups = d_ssm // SSM_GROUP
    assert seq_len % (ATTN_Q_BLOCKS * MOBA_BLOCK) == 0 and seq_len % SSM_CHUNK == 0
    assert d_ssm % d_attn == 0
    nb = seq_len // MOBA_BLOCK
    nc = seq_len // SSM_CHUNK
    n_rows = bsz * seq_len
    row = lambda v: v.reshape(1, -1)

    w_in_bf = w_in.astype(BF16)
    w_glu_bf = w_glu.astype(BF16)
    w_us_bf = w_up_ssm.astype(BF16)
    w_ua_bf = w_up_attn.astype(BF16)
    w_out_bf = w_out.astype(BF16)
    w_fu_bf = w_ffn_up.astype(BF16)
    w_fd_bf = w_ffn_down.astype(BF16)
    conv_b3 = conv_b[:, None, :]
    tables = jax.vmap(_ssm_tables)(lam_re, lam_im, log_dt, b_re, b_im, c_re, c_im, d_skip)
    dvec = d_skip.reshape(depth, d_ssm // LANES, 1, LANES)
    col_scale = jnp.ones((w_in.shape[2],), F32).at[d_ssm:d_ssm + d_attn].set(1.0 / math.sqrt(HEAD_DIM))

    x2 = x.reshape(n_rows, d)
    for l in range(depth):
        proj2, hbar = _inproj(x2, row(g_pre_mix[l]), w_in_bf, l, row(col_scale))

        hbar = hbar.reshape(bsz, nb, 1, d)
        hbar_t = jnp.broadcast_to(hbar, (bsz, nb, n_heads, d)).reshape(bsz * nb * n_heads, d)
        kbd = _kmean(hbar_t, w_in, l, nb, d_ssm, d_attn).reshape(bsz, n_heads * nb, d_attn)
        z = _gatevec(w_in, l, kbd, d_ssm)
        bias = _select(x2.reshape(bsz, seq_len, d), row(g_pre_mix[l]), z, nb)
        att = _attention(proj2.reshape(bsz, seq_len, -1), bias, nb, d_ssm)

        y2 = _ssm(proj2, tables, dvec, l, bsz, d_ssm)

        x2 = _merge(y2, att.reshape(n_rows, d_attn), proj2, x2, w_glu_bf, w_us_bf, w_ua_bf,
                    w_out_bf, l, row(g_post_mix[l]))
        x2 = _ffn(x2, row(g_pre_ffn[l]), w_fu_bf, conv_w, conv_b3, w_fd_bf, l,
                  row(g_post_ffn[l]), seq_len)
    return x2.reshape(bsz, seq_len, d)
```

```python
import functools
import math

import jax
import jax.numpy as jnp
from jax import lax
from jax.experimental import pallas as pl
from jax.experimental.pallas import tpu as pltpu

F32 = jnp.float32
BF16 = jnp.bfloat16

EPS = 1e-6
NEG = -1e30
SSM_GROUP = 16
HEAD_DIM = 128
MOBA_BLOCK = 256
MOBA_TOPK = 3
CONV_W = 3

LANES = 128
SUBLANES = 8
VMEM_LIMIT_BYTES = 56 * 1024 * 1024

SSM_CHUNK = 32
SSM_LANE_GROUPS = LANES // SSM_GROUP
SSM_STATE_POS = 8
ROW_TILE_PROJ = 1024
COL_TILE_PROJ = 2048
ROW_TILE_GATE = 512
ROW_TILE_MERGE = 512
MERGE_STRIP = 256
ROW_TILE_FFN = 512
COL_TILE_FFN = 512
FFN_STRIP = 256
K_TILE_GATE = 512
NORM_ROWS = 128
ATTN_Q_BLOCKS = 2
ATTN_HEADS = 2

_NT = (((1,), (1,)), ((), ()))
_HI = lax.Precision.HIGHEST


def _params(n_axes):
    return pltpu.CompilerParams(
        dimension_semantics=("arbitrary",) * n_axes,
        vmem_limit_bytes=VMEM_LIMIT_BYTES,
    )


def _rmsnorm_rows(x, g):
    ms = jnp.mean(x * x, axis=-1, keepdims=True)
    return x * lax.rsqrt(ms + EPS) * g


def _rmsnorm_into(h_ref, x_ref, g_ref):
    step_rows = min(NORM_ROWS, x_ref.shape[0])

    def step(r, carry):
        r0 = pl.multiple_of(r * step_rows, step_rows)
        h_ref[pl.ds(r0, step_rows), :] = _rmsnorm_rows(
            x_ref[pl.ds(r0, step_rows), :], g_ref[...]).astype(h_ref.dtype)
        return carry

    lax.fori_loop(0, x_ref.shape[0] // step_rows, step, 0)


def _inproj_kernel(x_ref, g_ref, w_ref, cs_ref, o_ref, hb_ref, h_ref):
    @pl.when(pl.program_id(1) == 0)
    def _():
        def step(r, carry):
            r0 = pl.multiple_of(r * MOBA_BLOCK, MOBA_BLOCK)
            hn = _rmsnorm_rows(x_ref[pl.ds(r0, MOBA_BLOCK), :], g_ref[...])
            h_ref[pl.ds(r0, MOBA_BLOCK), :] = hn.astype(h_ref.dtype)
            hb_ref[r] = jnp.mean(hn, axis=0, keepdims=True)
            return carry

        lax.fori_loop(0, x_ref.shape[0] // MOBA_BLOCK, step, 0)

    acc = jnp.dot(h_ref[...], w_ref[...], preferred_element_type=F32)
    o_ref[...] = (acc * cs_ref[...]).astype(o_ref.dtype)


def _inproj(x2, g, w_bf, layer, col_scale):
    n_rows, d = x2.shape
    d_in = w_bf.shape[2]
    tm = min(ROW_TILE_PROJ, n_rows)
    tn = COL_TILE_PROJ
    assert tm % MOBA_BLOCK == 0
    blocks_per_tile = tm // MOBA_BLOCK
    return pl.pallas_call(
        _inproj_kernel,
        grid=(n_rows // tm, d_in // tn),
        in_specs=[
            pl.BlockSpec((tm, d), lambda i, j: (i, 0)),
            pl.BlockSpec((1, d), lambda i, j: (0, 0)),
            pl.BlockSpec((None, d, tn), lambda i, j: (layer, 0, j)),
            pl.BlockSpec((1, tn), lambda i, j: (0, j)),
        ],
        out_specs=[
            pl.BlockSpec((tm, tn), lambda i, j: (i, j)),
            pl.BlockSpec((blocks_per_tile, 1, d), lambda i, j: (i, 0, 0)),
        ],
        out_shape=[
            jax.ShapeDtypeStruct((n_rows, d_in), BF16),
            jax.ShapeDtypeStruct((n_rows // MOBA_BLOCK, 1, d), F32),
        ],
        scratch_shapes=[pltpu.VMEM((tm, d), BF16)],
        compiler_params=_params(2),
        name="inproj",
    )(x2, g, w_bf, col_scale)


def _kmean_kernel(hb_ref, wk_ref, o_ref, *, nb):
    @pl.when(pl.program_id(0) == 0)
    def _():
        o_ref[...] = jnp.zeros_like(o_ref)

    o_ref[...] += jnp.dot(hb_ref[...], wk_ref[...], preferred_element_type=F32, precision=_HI)

    @pl.when(pl.program_id(0) == pl.num_programs(0) - 1)
    def _():
        r = lax.broadcasted_iota(jnp.int32, o_ref.shape, 0)
        c = lax.broadcasted_iota(jnp.int32, o_ref.shape, 1)
        n_heads = o_ref.shape[1] // HEAD_DIM
        keep = (c // HEAD_DIM) == (r % n_heads)
        o_ref[...] = jnp.where(keep, o_ref[...], 0.0)


def _kmean(hbar_t, w_in, layer, nb, d_ssm, d_attn):
    rows, d = hbar_t.shape
    k_col_block = (d_ssm + d_attn) // d_attn
    tk = K_TILE_GATE
    return pl.pallas_call(
        functools.partial(_kmean_kernel, nb=nb),
        grid=(d // tk,),
        in_specs=[
            pl.BlockSpec((rows, tk), lambda kk: (0, kk)),
            pl.BlockSpec((None, tk, d_attn), lambda kk: (layer, kk, k_col_block)),
        ],
        out_specs=pl.BlockSpec((rows, d_attn), lambda kk: (0, 0)),
        out_shape=jax.ShapeDtypeStruct((rows, d_attn), F32),
        compiler_params=_params(1),
        name="kmean",
    )(hbar_t, w_in)


def _gatevec_kernel(wq_ref, kbd_ref, o_ref):
    o_ref[0] = lax.dot_general(wq_ref[...], kbd_ref[0], _NT, preferred_element_type=F32,
                               precision=_HI)


def _gatevec(w_in, layer, kbd, d_ssm):
    bsz, n_gate, d_attn = kbd.shape
    d = w_in.shape[1]
    q_col_block = d_ssm // d_attn
    tk = K_TILE_GATE
    return pl.pallas_call(
        _gatevec_kernel,
        grid=(bsz, d // tk),
        in_specs=[
            pl.BlockSpec((None, tk, d_attn), lambda b, kk: (layer, kk, q_col_block)),
            pl.BlockSpec((1, n_gate, d_attn), lambda b, kk: (b, 0, 0)),
        ],
        out_specs=pl.BlockSpec((1, tk, n_gate), lambda b, kk: (b, kk, 0)),
        out_shape=jax.ShapeDtypeStruct((bsz, d, n_gate), F32),
        compiler_params=_params(2),
        name="gatevec",
    )(w_in, kbd)


def _select_kernel(x_ref, g_ref, z_ref, o_ref, *, nb, tq):
    h = _rmsnorm_rows(x_ref[0], g_ref[...])
    gate = jnp.dot(h, z_ref[0], preferred_element_type=F32, precision=_HI)
    n_heads = gate.shape[1] // nb
    row = lax.broadcasted_iota(jnp.int32, gate.shape, 0) + pl.program_id(1) * tq
    qblk = row // MOBA_BLOCK
    j = lax.broadcasted_iota(jnp.int32, gate.shape, 1) // n_heads
    past = j < qblk
    gate = jnp.where(past, gate, NEG)
    cnt = jnp.zeros(gate.shape, F32)
    for r in range(1, nb):
        other = pltpu.roll(gate, r * n_heads, 1)
        wins_tie = jnp.where(other >= gate, 1.0, 0.0)
        wins_strict = jnp.where(other > gate, 1.0, 0.0)
        cnt = cnt + jnp.where(j >= r, wins_tie, wins_strict)
    keep = jnp.logical_or(jnp.logical_and(past, cnt < float(MOBA_TOPK)), j == qblk)
    o_ref[0] = jnp.where(keep, 0.0, NEG)


def _select(x3, g, z, nb):
    bsz, s, d = x3.shape
    n_gate = z.shape[2]
    tq = min(ROW_TILE_GATE, s)
    return pl.pallas_call(
        functools.partial(_select_kernel, nb=nb, tq=tq),
        grid=(bsz, s // tq),
        in_specs=[
            pl.BlockSpec((1, tq, d), lambda b, i: (b, i, 0)),
            pl.BlockSpec((1, d), lambda b, i: (0, 0)),
            pl.BlockSpec((1, d, n_gate), lambda b, i: (b, 0, 0)),
        ],
        out_specs=pl.BlockSpec((1, tq, n_gate), lambda b, i: (b, i, 0)),
        out_shape=jax.ShapeDtypeStruct((bsz, s, n_gate), F32),
        compiler_params=_params(2),
        name="select",
    )(x3, g, z)


def _attn_kernel(q_ref, k_ref, v_ref, bias_ref, o_ref, qa_ref, s_ref, m_ref, acc_ref, *, nb):
    hp = pl.program_id(1)
    i = pl.program_id(2)
    blk = MOBA_BLOCK
    tq = q_ref.shape[1]
    n_gate = bias_ref.shape[2]
    heads = range(ATTN_HEADS)
    mask_rows = bias_ref[0].astype(BF16)
    for hh in heads:
        qa_ref[hh, :, :HEAD_DIM] = q_ref[0, :, hh * HEAD_DIM:(hh + 1) * HEAD_DIM]
        qa_ref[hh, :, HEAD_DIM:] = mask_rows
    m_ref[...] = jnp.full(m_ref.shape, -jnp.inf, F32)
    acc_ref[...] = jnp.zeros(acc_ref.shape, F32)
    lane = lax.broadcasted_iota(jnp.int32, (blk, n_gate), 1)
    ones = jnp.ones((blk, HEAD_DIM), BF16)

    def scores(slot, hh, jb):
        st = pl.multiple_of(jb * blk, blk)
        col = jb * (n_gate // nb) + hp * ATTN_HEADS + hh
        onehot = jnp.where(lane == col, 1.0, 0.0).astype(BF16)
        kj = k_ref[0, pl.ds(st, blk), hh * HEAD_DIM:(hh + 1) * HEAD_DIM]
        s_ref[slot, hh] = lax.dot_general(qa_ref[hh], jnp.concatenate([kj, onehot], axis=1), _NT,
                                          preferred_element_type=F32)

    def update(slot, hh, jb, causal):
        s = s_ref[slot, hh]
        if causal:
            qpos = lax.broadcasted_iota(jnp.int32, s.shape, 0) + i * tq
            kpos = lax.broadcasted_iota(jnp.int32, s.shape, 1) + jb * blk
            s = jnp.where(kpos <= qpos, s, NEG)
        st = pl.multiple_of(jb * blk, blk)
        m_old = m_ref[hh]
        m_new = jnp.maximum(m_old, jnp.max(s, axis=-1, keepdims=True))
        alpha = jnp.exp(m_old - m_new)
        p = jnp.exp(s - jnp.concatenate([m_new] * (blk // LANES), axis=1))
        vj = v_ref[0, pl.ds(st, blk), hh * HEAD_DIM:(hh + 1) * HEAD_DIM]
        acc_ref[hh] = jnp.concatenate([alpha, alpha], axis=1) * acc_ref[hh] + jnp.dot(
            p.astype(BF16), jnp.concatenate([vj, ones], axis=1), preferred_element_type=F32)
        m_ref[hh] = m_new

    for hh in heads:
        scores(0, hh, 0)

    def pair(jj, carry):
        for hh in heads:
            scores(1, hh, 2 * jj + 1)
            update(0, hh, 2 * jj, False)
            scores(0, hh, 2 * jj + 2)
            update(1, hh, 2 * jj + 1, False)
        return carry

    lax.fori_loop(0, i, pair, 0)
    for hh in heads:
        scores(1, hh, 2 * i + 1)
        update(0, hh, 2 * i, True)
        update(1, hh, 2 * i + 1, True)
        o_ref[0, :, hh * HEAD_DIM:(hh + 1) * HEAD_DIM] = (
            acc_ref[hh, :, :HEAD_DIM] / acc_ref[hh, :, HEAD_DIM:]).astype(o_ref.dtype)


def _attention(proj3, bias, nb, d_ssm):
    bsz, s, _ = proj3.shape
    n_gate = bias.shape[2]
    n_heads = n_gate // nb
    d_attn = n_heads * HEAD_DIM
    q0 = d_ssm // HEAD_DIM
    k0 = q0 + n_heads
    v0 = k0 + n_heads
    assert ATTN_Q_BLOCKS == 2 and nb % ATTN_Q_BLOCKS == 0
    assert n_heads % ATTN_HEADS == 0 and q0 % ATTN_HEADS == 0
    tq = ATTN_Q_BLOCKS * MOBA_BLOCK
    hw = ATTN_HEADS * HEAD_DIM
    return pl.pallas_call(
        functools.partial(_attn_kernel, nb=nb),
        grid=(bsz, n_heads // ATTN_HEADS, s // tq),
        in_specs=[
            pl.BlockSpec((1, tq, hw), lambda b, h, i: (b, i, q0 // ATTN_HEADS + h)),
            pl.BlockSpec((1, s, hw), lambda b, h, i: (b, 0, k0 // ATTN_HEADS + h)),
            pl.BlockSpec((1, s, hw), lambda b, h, i: (b, 0, v0 // ATTN_HEADS + h)),
            pl.BlockSpec((1, tq, n_gate), lambda b, h, i: (b, i, 0)),
        ],
        out_specs=pl.BlockSpec((1, tq, hw), lambda b, h, i: (b, i, h)),
        out_shape=jax.ShapeDtypeStruct((bsz, s, d_attn), BF16),
        scratch_shapes=[
            pltpu.VMEM((ATTN_HEADS, tq, HEAD_DIM + n_gate), BF16),
            pltpu.VMEM((2, ATTN_HEADS, tq, MOBA_BLOCK), F32),
            pltpu.VMEM((ATTN_HEADS, tq, LANES), F32),
            pltpu.VMEM((ATTN_HEADS, tq, 2 * HEAD_DIM), F32),
        ],
        compiler_params=_params(3),
        name="moba_attn",
    )(proj3, proj3, proj3, bias)


def _ssm_tables(lam_re, lam_im, log_dt, b_re, b_im, c_re, c_im, d_skip):
    t_len = SSM_CHUNK
    n_groups, n_state = lam_re.shape
    n_ch = SSM_GROUP
    assert 2 * n_state == LANES
    w = t_len * n_ch
    dt = jnp.exp(log_dt)[:, None]
    ar = lam_re * dt
    ai = lam_im * dt
    steps = jnp.arange(t_len + 1, dtype=F32)[None, :, None]
    mag = jnp.exp(ar[:, None, :] * steps)
    pw_re = mag * jnp.cos(ai[:, None, :] * steps)
    pw_im = mag * jnp.sin(ai[:, None, :] * steps)
    e1 = jnp.expm1(ar)
    sh = jnp.sin(0.5 * ai)
    num_re = e1 * jnp.cos(ai) - 2.0 * sh * sh
    num_im = (e1 + 1.0) * jnp.sin(ai)
    den = lam_re * lam_re + lam_im * lam_im
    coef_re = (num_re * lam_re + num_im * lam_im) / den
    coef_im = (num_im * lam_re - num_re * lam_im) / den
    bb_re = coef_re[..., None] * b_re - coef_im[..., None] * b_im
    bb_im = coef_re[..., None] * b_im + coef_im[..., None] * b_re
    cp_re = c_re[:, None] * pw_re[:, :, None, :] - c_im[:, None] * pw_im[:, :, None, :]
    cp_im = c_re[:, None] * pw_im[:, :, None, :] + c_im[:, None] * pw_re[:, :, None, :]
    cpow = jnp.concatenate([cp_re, -cp_im], axis=-1).reshape(n_groups, (t_len + 1) * n_ch, LANES)
    lane_of = (jnp.arange(n_groups) % SSM_LANE_GROUPS)[:, None] * n_ch + jnp.arange(n_ch)[None, :]
    place = (lane_of[:, :, None] == jnp.arange(LANES)[None, None, :]).astype(F32)
    bshift = jnp.einsum('gpm,gml->gpl', jnp.concatenate([bb_re, bb_im], axis=1), place, precision=_HI)
    rev_re = pw_re[:, :t_len][:, ::-1][:, :, None, :]
    rev_im = pw_im[:, :t_len][:, ::-1][:, :, None, :]
    bt_re = bb_re.transpose(0, 2, 1)[:, None]
    bt_im = bb_im.transpose(0, 2, 1)[:, None]
    pin = jnp.concatenate([rev_re * bt_re - rev_im * bt_im, rev_re * bt_im + rev_im * bt_re], axis=-1)
    pad_p = LANES - n_state
    pin = pin.reshape(n_groups, w, LANES)
    adec = jnp.stack([pw_re[:, t_len], pw_im[:, t_len]], axis=1)[:, :, None, :]
    adec = jnp.pad(adec, ((0, 0), (0, 0), (0, 0), (0, pad_p)))
    return cpow, bshift, pin.astype(BF16), adec


def _ssm_kernel(x_ref, cpow_ref, bshift_ref, pin_ref, adec_ref, dvec_ref, o_ref,
                rs_ref, xall_ref, hre_ref, him_ref, pre_ref, pim_ref, kk_ref, wt_ref, wrev_ref,
                pbd_ref, qbd_ref, *, bsz, nc):
    t_len = SSM_CHUNK
    n_pairs = t_len // 2
    ng = SSM_LANE_GROUPS
    gw = SSM_GROUP
    n_state = LANES // 2
    n_chunks = bsz * nc
    wide = ng * LANES

    for g in range(ng):
        kk_ref[g] = jnp.dot(cpow_ref[g, 0:t_len * gw, :], bshift_ref[g], preferred_element_type=F32,
                            precision=_HI)
    row = lax.broadcasted_iota(jnp.int32, (LANES, LANES), 0)
    col = lax.broadcasted_iota(jnp.int32, (LANES, LANES), 1)
    wt_ref[0] = jnp.zeros((LANES, LANES), BF16)
    for l in range(t_len):
        tile = kk_ref[:, l * gw:(l + 1) * gw, :].reshape(LANES, LANES)
        if l == 0:
            tile = tile + jnp.where(row == col, dvec_ref[0], 0.0)
        wt_ref[l + 1] = tile.astype(BF16)
    pw = 2 * LANES
    for d in range(n_pairs):
        c0 = (n_pairs - 1 - d) * pw
        wrev_ref[0:LANES, c0:c0 + LANES] = wt_ref[2 * d + 1]
        wrev_ref[0:LANES, c0 + LANES:c0 + pw] = wt_ref[2 * d]
        wrev_ref[LANES:, c0:c0 + LANES] = wt_ref[2 * d + 2]
        wrev_ref[LANES:, c0 + LANES:c0 + pw] = wt_ref[2 * d + 1]

    rs_ref[...] = x_ref[...].astype(F32)
    for s in range(t_len):
        xall_ref[:, s * LANES:(s + 1) * LANES] = rs_ref[pl.ds(s, n_chunks, stride=t_len), :].astype(BF16)

    pbd_ref[...] = jnp.zeros(pbd_ref.shape, BF16)
    for k in range(t_len // SSM_STATE_POS):
        for sl in range(SSM_STATE_POS):
            s0 = (k * SSM_STATE_POS + sl) * gw
            for g in range(ng):
                rows = slice(sl * LANES + g * gw, sl * LANES + (g + 1) * gw)
                lanes = slice(g * LANES, (g + 1) * LANES)
                pbd_ref[rows, lanes] = pin_ref[g, s0:s0 + gw, :]
        xk = xall_ref[:, k * SSM_STATE_POS * LANES:(k + 1) * SSM_STATE_POS * LANES]
        h_cat = jnp.dot(xk, pbd_ref[...], preferred_element_type=F32)
        if k == 0:
            hre_ref[...] = h_cat
        else:
            hre_ref[...] += h_cat
    lower = (lax.broadcasted_iota(jnp.int32, (n_chunks, wide), 1) % LANES) < n_state
    h_cat = hre_ref[...]
    him_ref[...] = jnp.where(lower, pltpu.roll(h_cat, wide - n_state, 1), 0.0)
    hre_ref[...] = jnp.where(lower, h_cat, 0.0)

    a_re = jnp.concatenate([adec_ref[g, 0] for g in range(ng)], axis=1)
    a_im = jnp.concatenate([adec_ref[g, 1] for g in range(ng)], axis=1)

    pre_ref[...] = jnp.zeros(pre_ref.shape, F32)
    pim_ref[...] = jnp.zeros(pim_ref.shape, F32)
    for g in range(ng):
        lanes = slice(g * LANES, (g + 1) * LANES)
        for b in range(bsz):
            pre_ref[g, pl.ds(b, nc, stride=SUBLANES), :] = hre_ref[b * nc:(b + 1) * nc, lanes]
            pim_ref[g, pl.ds(b, nc, stride=SUBLANES), :] = him_ref[b * nc:(b + 1) * nc, lanes]
    a_re3 = jnp.stack([adec_ref[g, 0] for g in range(ng)], axis=0)
    a_im3 = jnp.stack([adec_ref[g, 1] for g in range(ng)], axis=0)

    def step(c, carry):
        s_re, s_im = carry
        rows = pl.ds(pl.multiple_of(c * SUBLANES, SUBLANES), SUBLANES)
        loc_re = pre_ref[:, rows, :]
        loc_im = pim_ref[:, rows, :]
        pre_ref[:, rows, :] = s_re
        pim_ref[:, rows, :] = s_im
        return (a_re3 * s_re - a_im3 * s_im + loc_re, a_re3 * s_im + a_im3 * s_re + loc_im)

    zero = jnp.zeros((ng, SUBLANES, LANES), F32)
    lax.fori_loop(0, nc, step, (zero, zero), unroll=2)
    for g in range(ng):
        lanes = slice(g * LANES, (g + 1) * LANES)
        for b in range(bsz):
            hre_ref[b * nc:(b + 1) * nc, lanes] = pre_ref[g, pl.ds(b, nc, stride=SUBLANES), :]
            him_ref[b * nc:(b + 1) * nc, lanes] = pim_ref[g, pl.ds(b, nc, stride=SUBLANES), :]

    h_in = (hre_ref[...] + pltpu.roll(him_ref[...], n_state, 1)).astype(BF16)
    qbd_ref[...] = jnp.zeros(qbd_ref.shape, BF16)
    for q in range(n_pairs):
        for tl in range(2):
            s0 = (2 * q + tl + 1) * gw
            for g in range(ng):
                qbd_ref[tl * LANES + g * gw:tl * LANES + (g + 1) * gw, g * LANES:(g + 1) * LANES] = (
                    cpow_ref[g, s0:s0 + gw, :].astype(BF16))
        y = lax.dot_general(h_in, qbd_ref[...], _NT, preferred_element_type=F32)
        y = y + lax.dot_general(xall_ref[:, 0:(q + 1) * pw], wrev_ref[:, (n_pairs - 1 - q) * pw:], _NT,
                                preferred_element_type=F32)
        y = jax.nn.gelu(y)
        rs_ref[pl.ds(2 * q, n_chunks, stride=t_len), :] = y[:, 0:LANES]
        rs_ref[pl.ds(2 * q + 1, n_chunks, stride=t_len), :] = y[:, LANES:]
    o_ref[...] = rs_ref[...].astype(o_ref.dtype)


def _ssm(proj2, tables, dvec, layer, bsz, d_ssm):
    cpow, bshift, pin, adec = tables
    rows = proj2.shape[0]
    nc = rows // bsz // SSM_CHUNK
    n_chunks = bsz * nc
    ng = SSM_LANE_GROUPS
    n_pairs = SSM_CHUNK // 2
    wide = ng * LANES
    cp_rows = cpow.shape[2]
    w = SSM_CHUNK * SSM_GROUP
    return pl.pallas_call(
        functools.partial(_ssm_kernel, bsz=bsz, nc=nc),
        grid=(d_ssm // LANES,),
        in_specs=[
            pl.BlockSpec((rows, LANES), lambda j: (0, j), pipeline_mode=pl.Buffered(1)),
            pl.BlockSpec((None, ng, cp_rows, LANES), lambda j: (layer, j, 0, 0)),
            pl.BlockSpec((None, ng, LANES, LANES), lambda j: (layer, j, 0, 0)),
            pl.BlockSpec((None, ng, w, LANES), lambda j: (layer, j, 0, 0)),
            pl.BlockSpec((None, ng, 2, 1, LANES), lambda j: (layer, j, 0, 0, 0)),
            pl.BlockSpec((None, 1, 1, LANES), lambda j: (layer, j, 0, 0)),
        ],
        out_specs=pl.BlockSpec((rows, LANES), lambda j: (0, j)),
        out_shape=jax.ShapeDtypeStruct((rows, d_ssm), BF16),
        scratch_shapes=[
            pltpu.VMEM((rows, LANES), F32),
            pltpu.VMEM((n_chunks, SSM_CHUNK * LANES), BF16),
            pltpu.VMEM((n_chunks, wide), F32),
            pltpu.VMEM((n_chunks, wide), F32),
            pltpu.VMEM((ng, nc * SUBLANES, LANES), F32),
            pltpu.VMEM((ng, nc * SUBLANES, LANES), F32),
            pltpu.VMEM((ng, w, LANES), F32),
            pltpu.VMEM((SSM_CHUNK + 1, LANES, LANES), BF16),
            pltpu.VMEM((2 * LANES, n_pairs * 2 * LANES), BF16),
            pltpu.VMEM((SSM_STATE_POS * LANES, wide), BF16),
            pltpu.VMEM((2 * LANES, wide), BF16),
        ],
        compiler_params=_params(1),
        name="s5_chunked",
    )(proj2, cpow, bshift, pin, adec, dvec)


def _merge_kernel(y_ref, att_ref, ga_ref, gb_ref, x_ref, wglu_ref, wus_ref, wua_ref, wout_ref,
                  g_ref, o_ref):
    tm = x_ref.shape[0]
    strip = min(MERGE_STRIP, tm)
    for r0 in range(0, tm, strip):
        rows = slice(r0, r0 + strip)
        y = y_ref[rows, :]
        z = jnp.dot(y, wglu_ref[...], preferred_element_type=F32)
        ya = jnp.dot(att_ref[rows, :], wua_ref[...], preferred_element_type=F32)
        s5 = (y.astype(F32) * jax.nn.sigmoid(z)).astype(BF16)
        ys = jnp.dot(s5, wus_ref[...], preferred_element_type=F32)
        m = (jax.nn.sigmoid(ga_ref[rows, :].astype(F32)) * ys
             + jax.nn.sigmoid(gb_ref[rows, :].astype(F32)) * ya)
        o = jnp.dot(m.astype(BF16), wout_ref[...], preferred_element_type=F32)
        o_ref[rows, :] = x_ref[rows, :] + _rmsnorm_rows(o, g_ref[...])


def _merge(y2, att2, proj2, x2, w_glu, w_us, w_ua, w_out, layer, g_post):
    n_rows, d = x2.shape
    d_ssm = y2.shape[1]
    d_attn = att2.shape[1]
    tm = min(ROW_TILE_MERGE, n_rows)
    ga_blk = (d_ssm + 3 * d_attn) // d

    def weight(wt):
        return pl.BlockSpec((None,) + wt.shape[1:], lambda i: (layer, 0, 0),
                            pipeline_mode=pl.Buffered(1))

    return pl.pallas_call(
        _merge_kernel,
        grid=(n_rows // tm,),
        in_specs=[
            pl.BlockSpec((tm, d_ssm), lambda i: (i, 0)),
            pl.BlockSpec((tm, d_attn), lambda i: (i, 0)),
            pl.BlockSpec((tm, d), lambda i: (i, ga_blk)),
            pl.BlockSpec((tm, d), lambda i: (i, ga_blk + 1)),
            pl.BlockSpec((tm, d), lambda i: (i, 0)),
            weight(w_glu), weight(w_us), weight(w_ua), weight(w_out),
            pl.BlockSpec((1, d), lambda i: (0, 0)),
        ],
        out_specs=pl.BlockSpec((tm, d), lambda i: (i, 0)),
        out_shape=jax.ShapeDtypeStruct((n_rows, d), F32),
        compiler_params=_params(1),
        name="merge",
    )(y2, att2, proj2, proj2, x2, w_glu, w_us, w_ua, w_out, g_post)


def _ffn_kernel(x_ref, gpre_ref, wa_ref, wv_ref, cwa_ref, cwv_ref, cba_ref, cbv_ref, wd_ref,
                gpost_ref, o_ref, h_ref, za_ref, zv_ref, halo_a_ref, halo_v_ref,
                *, tiles_per_seq):
    i = pl.program_id(0)
    c = pl.program_id(1)
    tm = x_ref.shape[0]

    @pl.when(c == 0)
    def _():
        _rmsnorm_into(h_ref, x_ref, gpre_ref)
        o_ref[...] = jnp.zeros_like(o_ref)

    @pl.when((i % tiles_per_seq) == 0)
    def _():
        halo_a_ref[c] = jnp.zeros(halo_a_ref.shape[1:], F32)
        halo_v_ref[c] = jnp.zeros(halo_v_ref.shape[1:], F32)

    za_ref[0:SUBLANES, :] = halo_a_ref[c]
    zv_ref[0:SUBLANES, :] = halo_v_ref[c]
    cwa, cwv = cwa_ref[...], cwv_ref[...]
    cba, cbv = cba_ref[...], cbv_ref[...]

    def up(r0):
        h = h_ref[r0:r0 + FFN_STRIP, :]
        out_rows = slice(SUBLANES + r0, SUBLANES + r0 + FFN_STRIP)
        za_ref[out_rows, :] = jnp.dot(h, wa_ref[...], preferred_element_type=F32)
        zv_ref[out_rows, :] = jnp.dot(h, wv_ref[...], preferred_element_type=F32)

    def conv_strip(z_ref, cw, bias, r0):
        base = SUBLANES + r0
        z0 = z_ref[base:base + FFN_STRIP, :]
        z1 = z_ref[base - 1:base - 1 + FFN_STRIP, :]
        z2 = z_ref[base - 2:base - 2 + FFN_STRIP, :]
        return cw[0:1, :] * z2 + cw[1:2, :] * z1 + cw[2:3, :] * z0 + bias

    def finish(r0):
        a = conv_strip(za_ref, cwa, cba, r0)
        v = conv_strip(zv_ref, cwv, cbv, r0)
        gated = (a * jax.nn.sigmoid(a) * v).astype(BF16)
        o_ref[r0:r0 + FFN_STRIP, :] += jnp.dot(gated, wd_ref[...], preferred_element_type=F32)

    strips = list(range(0, tm, FFN_STRIP))
    up(strips[0])
    for k, r0 in enumerate(strips):
        if k + 1 < len(strips):
            up(strips[k + 1])
        finish(r0)

    halo_a_ref[c] = za_ref[tm:tm + SUBLANES, :]
    halo_v_ref[c] = zv_ref[tm:tm + SUBLANES, :]

    @pl.when(c == pl.num_programs(1) - 1)
    def _():
        step_rows = min(NORM_ROWS, tm)

        def finalize(r, carry):
            rows = pl.ds(pl.multiple_of(r * step_rows, step_rows), step_rows)
            o_ref[rows, :] = x_ref[rows, :] + _rmsnorm_rows(o_ref[rows, :], gpost_ref[...])
            return carry

        lax.fori_loop(0, tm // step_rows, finalize, 0)


def _ffn(x2, g_pre, w_up, conv_w, conv_b, w_down, layer, g_post, seq_len):
    n_rows, d = x2.shape
    d_ff = w_down.shape[1]
    tm = min(ROW_TILE_FFN, seq_len)
    tf = COL_TILE_FFN
    nff = d_ff // tf
    assert tm % FFN_STRIP == 0
    const = lambda i, c: (0, 0)
    zbuf = pltpu.VMEM((SUBLANES + tm, tf), F32)
    return pl.pallas_call(
        functools.partial(_ffn_kernel, tiles_per_seq=seq_len // tm),
        grid=(n_rows // tm, nff),
        in_specs=[
            pl.BlockSpec((tm, d), lambda i, c: (i, 0)),
            pl.BlockSpec((1, d), const),
            pl.BlockSpec((None, d, tf), lambda i, c: (layer, 0, c)),
            pl.BlockSpec((None, d, tf), lambda i, c: (layer, 0, nff + c)),
            pl.BlockSpec((None, CONV_W, tf), lambda i, c: (layer, 0, c)),
            pl.BlockSpec((None, CONV_W, tf), lambda i, c: (layer, 0, nff + c)),
            pl.BlockSpec((None, 1, tf), lambda i, c: (layer, 0, c)),
            pl.BlockSpec((None, 1, tf), lambda i, c: (layer, 0, nff + c)),
            pl.BlockSpec((None, tf, d), lambda i, c: (layer, c, 0)),
            pl.BlockSpec((1, d), const),
        ],
        out_specs=pl.BlockSpec((tm, d), lambda i, c: (i, 0)),
        out_shape=jax.ShapeDtypeStruct((n_rows, d), F32),
        scratch_shapes=[
            pltpu.VMEM((tm, d), BF16),
            zbuf, zbuf,
            pltpu.VMEM((nff, SUBLANES, tf), F32),
            pltpu.VMEM((nff, SUBLANES, tf), F32),
        ],
        compiler_params=_params(2),
        name="convglu_ffn",
    )(x2, g_pre, w_up, w_up, conv_w, conv_w, conv_b, conv_b, w_down, g_post)


def kernel(x, g_pre_mix, w_in, lam_re, lam_im, log_dt, b_re, b_im, c_re, c_im, d_skip, w_glu,
           w_up_ssm, w_up_attn, w_out, g_post_mix, g_pre_ffn, w_ffn_up, conv_w, conv_b,
           w_ffn_down, g_post_ffn):
    bsz, seq_len, d = x.shape
    depth = w_in.shape[0]
    d_ssm = w_glu.shape[1]
    d_attn = w_up_attn.shape[1]
    n_heads = d_attn // HEAD_DIM
    assert seq_len % (ATTN_Q_BLOCKS * MOBA_BLOCK) == 0 and seq_len % SSM_CHUNK == 0
    assert d_ssm % d_attn == 0 and d_ssm % LANES == 0 and bsz <= SUBLANES
    nb = seq_len // MOBA_BLOCK
    n_rows = bsz * seq_len
    row = lambda v: v.reshape(1, -1)

    w_in_bf = w_in.astype(BF16)
    w_glu_bf = w_glu.astype(BF16)
    w_us_bf = w_up_ssm.astype(BF16)
    w_ua_bf = w_up_attn.astype(BF16)
    w_out_bf = w_out.astype(BF16)
    w_fu_bf = w_ffn_up.astype(BF16)
    w_fd_bf = w_ffn_down.astype(BF16)
    conv_b3 = conv_b[:, None, :]
    tables = jax.vmap(_ssm_tables)(lam_re, lam_im, log_dt, b_re, b_im, c_re, c_im, d_skip)
    dvec = d_skip.reshape(depth, d_ssm // LANES, 1, LANES)
    col_scale = jnp.ones((w_in.shape[2],), F32).at[d_ssm:d_ssm + d_attn].set(1.0 / math.sqrt(HEAD_DIM))

    x2 = x.reshape(n_rows, d)
    for l in range(depth):
        proj2, hbar = _inproj(x2, row(g_pre_mix[l]), w_in_bf, l, row(col_scale))

        hbar = hbar.reshape(bsz, nb, 1, d)
        hbar_t = jnp.broadcast_to(hbar, (bsz, nb, n_heads, d)).reshape(bsz * nb * n_heads, d)
        kbd = _kmean(hbar_t, w_in, l, nb, d_ssm, d_attn).reshape(bsz, n_heads * nb, d_attn)
        z = _gatevec(w_in, l, kbd, d_ssm)
        bias = _select(x2.reshape(bsz, seq_len, d), row(g_pre_mix[l]), z, nb)
        att = _attention(proj2.reshape(bsz, seq_len, -1), bias, nb, d_ssm)

        y2 = _ssm(proj2, tables, dvec, l, bsz, d_ssm)

        x2 = _merge(y2, att.reshape(n_rows, d_attn), proj2, x2, w_glu_bf, w_us_bf, w_ua_bf,
                    w_out_bf, l, row(g_post_mix[l]))
        x2 = _ffn(x2, row(g_pre_ffn[l]), w_fu_bf, conv_w, conv_b3, w_fd_bf, l,
                  row(g_post_ffn[l]), seq_len)
    return x2.reshape(bsz, seq_len, d)
```
